```python
import jax
import jax.numpy as jnp
from jax import lax
import numpy as np

D_MODEL = 2048
BATCH = 2
SEQ = 8192
DEPTH = 2

CHUNK = 64
LN_EPS = 1e-5
ALPHA = (2 * DEPTH) ** 0.25
BETA = (8 * DEPTH) ** -0.25
N_EVEN = (DEPTH + 1) // 2
N_ODD = DEPTH // 2

CONV_CH = D_MODEL // 2
CONV_WIDTH = 31
HGRN_HEADS = 8
HGRN_DK = 128
HGRN_DV = (D_MODEL // 2) // HGRN_HEADS
HGRN_KDIM = HGRN_HEADS * HGRN_DK
HGRN_VDIM = HGRN_HEADS * HGRN_DV
FOX_HEADS = 8
FOX_DH = 128
FOX_WIDTH = FOX_HEADS * FOX_DH
FOX_QBLOCK = 128
CA_HEADS = 8
CA_DH = 128
CA_WIDTH = CA_HEADS * CA_DH
CA_LEFT_CHUNKS = 8
CA_BAND = (CA_LEFT_CHUNKS + 1) * CHUNK
REL_CLIP = 256
REL_TABLE = (CHUNK - 1) + REL_CLIP + 1
D_FF = 4 * D_MODEL

EVEN_IN = 2 * CONV_CH + 2 * HGRN_KDIM + 2 * HGRN_VDIM
ODD_IN = 3 * FOX_WIDTH + FOX_HEADS + 3 * CA_WIDTH

kernel_name = 'hybrid_conv_hgrn2_fox_chunkattn_trunk'


def layer_norm(x, g, b):
    xf = x.astype(jnp.float32)
    mu = jnp.mean(xf, axis=-1, keepdims=True)
    var = jnp.mean(jnp.square(xf - mu), axis=-1, keepdims=True)
    return ((xf - mu) * lax.rsqrt(var + LN_EPS) * g + b).astype(x.dtype)


def conformer_conv(u, conv_w, conv_b, ln_g, ln_b):
    a, gate = jnp.split(u, 2, axis=-1)
    h = a * jax.nn.sigmoid(gate)
    h = lax.conv_general_dilated(
        h, conv_w[:, None, :].astype(h.dtype), window_strides=(1,),
        padding=[(CONV_WIDTH - 1, 0)],
        dimension_numbers=('NWC', 'WIO', 'NWC'),
        feature_group_count=CONV_CH) + conv_b
    return jax.nn.silu(layer_norm(h, ln_g, ln_b))


def hgrn2(q, f_logit, i, g, lb, gnorm_g):
    B, S, _ = q.shape
    nc = S // CHUNK
    f32 = jnp.float32
    f = lb + (1.0 - lb) * jax.nn.sigmoid(f_logit.astype(f32))
    log_f = jnp.log(f)
    k = 1.0 - f

    def to_chunks(t, d):
        return t.astype(f32).reshape(B, nc, CHUNK, HGRN_HEADS, d).transpose(1, 0, 3, 2, 4)

    qc = to_chunks(jax.nn.silu(q), HGRN_DK)
    kc = to_chunks(k, HGRN_DK)
    vc = to_chunks(i, HGRN_DV)
    lc = to_chunks(log_f, HGRN_DK)
    causal = jnp.tril(jnp.ones((CHUNK, CHUNK), dtype=bool))[:, :, None]

    def step(state, inp):
        qb, kb, vb, lfb = inp
        L = jnp.cumsum(lfb, axis=2)
        diff = L[:, :, :, None, :] - L[:, :, None, :, :]
        decay = jnp.exp(jnp.where(causal, diff, -jnp.inf))
        scores = jnp.einsum('bhtk,bhsk,bhtsk->bhts', qb, kb, decay)
        o = (jnp.einsum('bhts,bhsv->bhtv', scores, vb)
             + jnp.einsum('bhtk,bhkv->bhtv', qb * jnp.exp(L), state))
        L_end = L[:, :, -1, :]
        state = (jnp.exp(L_end)[..., None] * state
                 + jnp.einsum('bhsk,bhsv->bhkv', kb * jnp.exp(L_end[:, :, None, :] - L), vb))
        return state, o

    s0 = jnp.zeros((B, HGRN_HEADS, HGRN_DK, HGRN_DV), f32)
    _, o = lax.scan(step, s0, (qc, kc, vc, lc))
    o = o.transpose(1, 0, 3, 2, 4).reshape(B, S, HGRN_HEADS, HGRN_DV)
    o = o * lax.rsqrt(jnp.mean(jnp.square(o), axis=-1, keepdims=True) + LN_EPS)
    o = o.reshape(B, S, HGRN_VDIM) * gnorm_g * jax.nn.silu(g.astype(f32))
    return o.astype(q.dtype)


def forgetting_attention(q, k, v, f_logit):
    B, S, _ = q.shape
    nb = S // FOX_QBLOCK
    heads = lambda t: t.reshape(B, S, FOX_HEADS, FOX_DH).transpose(0, 2, 1, 3)
    qh, kh, vh = heads(q), heads(k), heads(v)
    F = jnp.cumsum(jax.nn.log_sigmoid(f_logit.astype(jnp.float32)), axis=1).transpose(0, 2, 1)
    q_blocks = qh.reshape(B, FOX_HEADS, nb, FOX_QBLOCK, FOX_DH).transpose(2, 0, 1, 3, 4)
    F_blocks = F.reshape(B, FOX_HEADS, nb, FOX_QBLOCK).transpose(2, 0, 1, 3)
    kpos = jnp.arange(S)
    scale = FOX_DH ** -0.5

    def block(args):
        qb, Fb, bidx = args
        s = jnp.einsum('bhqd,bhkd->bhqk', qb, kh).astype(jnp.float32) * scale
        s = s + Fb[..., None] - F[:, :, None, :]
        qpos = bidx * FOX_QBLOCK + jnp.arange(FOX_QBLOCK)
        s = jnp.where(kpos[None, :] <= qpos[:, None], s, -jnp.inf)
        p = jax.nn.softmax(s, axis=-1).astype(vh.dtype)
        return jnp.einsum('bhqk,bhkd->bhqd', p, vh)

    out = lax.map(block, (q_blocks, F_blocks, jnp.arange(nb)))
    return out.transpose(1, 0, 3, 2, 4).reshape(B, S, FOX_WIDTH)


def chunked_relpos_attention(q, k, v, rel_bias):
    B, S, _ = q.shape
    nc = S // CHUNK
    left = CA_LEFT_CHUNKS * CHUNK
    qc = q.reshape(B, nc, CHUNK, CA_HEADS, CA_DH).transpose(1, 0, 3, 2, 4)
    pad = lambda t: jnp.pad(t.reshape(B, S, CA_HEADS, CA_DH),
                            ((0, 0), (left, 0), (0, 0), (0, 0))).transpose(0, 2, 1, 3)
    kp, vp = pad(k), pad(v)
    qi = jnp.arange(CHUNK)
    km = jnp.arange(CA_BAND)
    rel = (left + qi[:, None]) - km[None, :]
    bias = rel_bias[:, jnp.minimum(rel, REL_CLIP) + (CHUNK - 1)].astype(jnp.float32)
    scale = CA_DH ** -0.5

    def chunk(args):
        qb, c = args
        start = c * CHUNK
        kb = lax.dynamic_slice_in_dim(kp, start, CA_BAND, axis=2)
        vb = lax.dynamic_slice_in_dim(vp, start, CA_BAND, axis=2)
        s = jnp.einsum('bhqd,bhkd->bhqk', qb, kb).astype(jnp.float32) * scale + bias
        s = jnp.where(start + km >= left, s, -jnp.inf)
        p = jax.nn.softmax(s, axis=-1).astype(vb.dtype)
        return jnp.einsum('bhqk,bhkd->bhqd', p, vb)

    out = lax.map(chunk, (qc, jnp.arange(nc)))
    return out.transpose(1, 0, 3, 2, 4).reshape(B, S, CA_WIDTH)


def setup_inputs(seed: int = 0) -> dict:
    key = jax.random.key(seed)
    ks = jax.random.split(key, 19)
    f32 = jnp.float32

    def nrm(k, shape, scale):
        return scale * jax.random.normal(k, shape, f32)

    return {
        'x': nrm(ks[0], (BATCH, SEQ, D_MODEL), 1.0),
        'ev_w_in': nrm(ks[1], (N_EVEN, D_MODEL, EVEN_IN), D_MODEL ** -0.5),
        'ev_conv_w': nrm(ks[2], (N_EVEN, CONV_WIDTH, CONV_CH), CONV_WIDTH ** -0.5),
        'ev_conv_b': nrm(ks[3], (N_EVEN, CONV_CH), 0.02),
        'ev_conv_ln_g': 1.0 + nrm(ks[4], (N_EVEN, CONV_CH), 0.1),
        'ev_conv_ln_b': nrm(ks[5], (N_EVEN, CONV_CH), 0.02),
        'hgrn_lb_logits': nrm(ks[6], (N_EVEN + 1, HGRN_KDIM), 0.1),
        'ev_gnorm_g': 1.0 + nrm(ks[7], (N_EVEN, HGRN_VDIM), 0.1),
        'ev_w_out': nrm(ks[8], (N_EVEN, CONV_CH + HGRN_VDIM, D_MODEL), (CONV_CH + HGRN_VDIM) ** -0.5 * BETA),
        'od_w_in': nrm(ks[9], (N_ODD, D_MODEL, ODD_IN), D_MODEL ** -0.5),
        'fox_b_f': 3.0 + nrm(ks[10], (N_ODD, FOX_HEADS), 0.5),
        'rel_bias': nrm(ks[11], (N_ODD, CA_HEADS, REL_TABLE), 0.5),
        'od_w_out': nrm(ks[12], (N_ODD, FOX_WIDTH + CA_WIDTH, D_MODEL), (FOX_WIDTH + CA_WIDTH) ** -0.5 * BETA),
        'ln_mix_g': 1.0 + nrm(ks[13], (DEPTH, D_MODEL), 0.1),
        'ln_mix_b': nrm(ks[14], (DEPTH, D_MODEL), 0.02),
        'mlp_w1': nrm(ks[15], (DEPTH, D_MODEL, D_FF), D_MODEL ** -0.5),
        'mlp_w2': nrm(ks[16], (DEPTH, D_FF, D_MODEL), D_FF ** -0.5 * BETA),
        'ln_mlp_g': 1.0 + nrm(ks[17], (DEPTH, D_MODEL), 0.1),
        'ln_mlp_b': nrm(ks[18], (DEPTH, D_MODEL), 0.02),
    }


def reference(x, ev_w_in, ev_conv_w, ev_conv_b, ev_conv_ln_g, ev_conv_ln_b, hgrn_lb_logits,
              ev_gnorm_g, ev_w_out, od_w_in, fox_b_f, rel_bias, od_w_out, ln_mix_g, ln_mix_b,
              mlp_w1, mlp_w2, ln_mlp_g, ln_mlp_b):
    lower_bounds = jnp.cumsum(jax.nn.softmax(hgrn_lb_logits.astype(jnp.float32), axis=0), axis=0)
    for l in range(DEPTH):
        j = l // 2
        if l % 2 == 0:
            u = x @ ev_w_in[j]
            conv_in, hq, hf, hi, hg = jnp.split(
                u, [2 * CONV_CH, 2 * CONV_CH + HGRN_KDIM, 2 * CONV_CH + 2 * HGRN_KDIM,
                    2 * CONV_CH + 2 * HGRN_KDIM + HGRN_VDIM], axis=-1)
            a_out = conformer_conv(conv_in, ev_conv_w[j], ev_conv_b[j], ev_conv_ln_g[j], ev_conv_ln_b[j])
            b_out = hgrn2(hq, hf, hi, hg, lower_bounds[j], ev_gnorm_g[j])
            mix = jnp.concatenate([a_out, b_out], axis=-1) @ ev_w_out[j]
        else:
            u = x @ od_w_in[j]
            c_q, c_k, c_v, c_f, d_q, d_k, d_v = jnp.split(
                u, [FOX_WIDTH, 2 * FOX_WIDTH, 3 * FOX_WIDTH, 3 * FOX_WIDTH + FOX_HEADS,
                    3 * FOX_WIDTH + FOX_HEADS + CA_WIDTH, 3 * FOX_WIDTH + FOX_HEADS + 2 * CA_WIDTH], axis=-1)
            c_out = forgetting_attention(c_q, c_k, c_v, c_f + fox_b_f[j])
            d_out = chunked_relpos_attention(d_q, d_k, d_v, rel_bias[j])
            mix = jnp.concatenate([c_out, d_out], axis=-1) @ od_w_out[j]
        x = layer_norm(ALPHA * x + mix, ln_mix_g[l], ln_mix_b[l])
        h = jnp.square(jax.nn.relu(x @ mlp_w1[l])) @ mlp_w2[l]
        x = layer_norm(ALPHA * x + h, ln_mlp_g[l], ln_mlp_b[l])
    return x
```

```python
import functools

import jax
import jax.numpy as jnp
from jax import lax
from jax.experimental import pallas as pl
from jax.experimental.pallas import tpu as pltpu

F32 = jnp.float32
BF16 = jnp.bfloat16

D_MODEL = 2048
DEPTH = 2
CHUNK = 64
LN_EPS = 1e-5
ALPHA = (2 * DEPTH) ** 0.25
HEADS = 8
HEAD_DIM = 128
BRANCH = HEADS * HEAD_DIM
CONV_WIDTH = 31
CONV_HALO = 32
CA_LEFT_CHUNKS = 8
REL_CLIP = 256
D_FF = 4 * D_MODEL
NEG = -1e30

LANES = 128
SUBLANES = 8
VMEM_LIMIT = 56 * 1024 * 1024

NT_DIMS = (((1,), (1,)), ((), ()))
TN_DIMS = (((0,), (0,)), ((), ()))


def _params(*sem):
    return pltpu.CompilerParams(dimension_semantics=sem, vmem_limit_bytes=VMEM_LIMIT)


def _sigmoid(x):
    return 1.0 / (1.0 + jnp.exp(-x))


def _layer_norm_rows(y, g, b):
    mu = jnp.mean(y, axis=-1, keepdims=True)
    d = y - mu
    var = jnp.mean(d * d, axis=-1, keepdims=True)
    return d * lax.rsqrt(var + LN_EPS) * g + b


def _split3(x):
    hi = x.astype(BF16)
    r = x - hi.astype(F32)
    mid = r.astype(BF16)
    lo = (r - mid.astype(F32)).astype(BF16)
    return hi, mid, lo


def _inproj_kernel(x_ref, w_ref, s_ref, o_ref, xb_ref):
    @pl.when(pl.program_id(1) == 0)
    def _():
        xb_ref[...] = x_ref[...].astype(BF16)

    acc = jnp.dot(xb_ref[...], w_ref[...], preferred_element_type=F32)
    o_ref[...] = (acc * s_ref[...]).astype(o_ref.dtype)


def inproj(x, w, colscale, out_dtype, bm=1024, bn=1024):
    t, k = x.shape
    n = w.shape[1]
    return pl.pallas_call(
        _inproj_kernel,
        grid=(t // bm, n // bn),
        in_specs=[
            pl.BlockSpec((bm, k), lambda i, j: (i, 0)),
            pl.BlockSpec((k, bn), lambda i, j: (0, j)),
            pl.BlockSpec((1, bn), lambda i, j: (0, j)),
        ],
        out_specs=pl.BlockSpec((bm, bn), lambda i, j: (i, j)),
        out_shape=jax.ShapeDtypeStruct((t, n), out_dtype),
        scratch_shapes=[pltpu.VMEM((bm, k), BF16)],
        compiler_params=_params("parallel", "arbitrary"),
        name="inproj",
    )(x, w, colscale)


CONV_ROWS = 32


def _conv_kernel(a_ref, g_ref, ah_ref, gh_ref, w_ref, cb_ref, lg_ref, lb_ref, o_ref, hext_ref, hs_ref, y_ref, *, bt):
    i = pl.program_id(1)
    halo = ah_ref[...] * _sigmoid(gh_ref[...])
    hext_ref[0:CONV_HALO, :] = jnp.where(i > 0, halo, 0.0)
    hext_ref[CONV_HALO:CONV_HALO + bt, :] = a_ref[...] * _sigmoid(g_ref[...])
    span = bt + CONV_HALO - SUBLANES
    for d in range(1, SUBLANES):
        hs_ref[d - 1, 0:span, :] = hext_ref[d:d + span, :]

    first = CONV_HALO - (CONV_WIDTH - 1)

    def rows(c, carry):
        base = pl.multiple_of(c * CONV_ROWS, CONV_ROWS)
        for cg in range(BRANCH // 256):
            cs = slice(cg * 256, (cg + 1) * 256)
            acc = jnp.zeros((CONV_ROWS, 256), F32) + cb_ref[:, cs]
            for j in range(CONV_WIDTH):
                off = first + j
                d, al = off % SUBLANES, off - off % SUBLANES
                if d == 0:
                    tap = hext_ref[pl.ds(base + al, CONV_ROWS), cs]
                else:
                    tap = hs_ref[d - 1, pl.ds(base + al, CONV_ROWS), cs]
                acc = acc + w_ref[j:j + 1, cs] * tap
            y_ref[pl.ds(base, CONV_ROWS), cs] = acc
        return carry

    lax.fori_loop(0, bt // CONV_ROWS, rows, 0)
    z = _layer_norm_rows(y_ref[...], lg_ref[...], lb_ref[...])
    o_ref[...] = (z * _sigmoid(z)).astype(o_ref.dtype)


def conv_branch(u, conv_w, conv_b, ln_g, ln_b, bt=512):
    b, s, _ = u.shape
    hb = bt // CONV_HALO
    w = jnp.pad(conv_w, ((0, CONV_HALO - CONV_WIDTH), (0, 0)))
    row = lambda v: v.reshape(1, BRANCH)
    vec = pl.BlockSpec((1, BRANCH), lambda bi, i: (0, 0))
    return pl.pallas_call(
        functools.partial(_conv_kernel, bt=bt),
        grid=(b, s // bt),
        in_specs=[
            pl.BlockSpec((None, bt, BRANCH), lambda bi, i: (bi, i, 0)),
            pl.BlockSpec((None, bt, BRANCH), lambda bi, i: (bi, i, 1)),
            pl.BlockSpec((None, CONV_HALO, BRANCH), lambda bi, i: (bi, jnp.maximum(i * hb - 1, 0), 0)),
            pl.BlockSpec((None, CONV_HALO, BRANCH), lambda bi, i: (bi, jnp.maximum(i * hb - 1, 0), 1)),
            pl.BlockSpec((CONV_HALO, BRANCH), lambda bi, i: (0, 0)),
            vec, vec, vec,
        ],
        out_specs=pl.BlockSpec((None, bt, BRANCH), lambda bi, i: (bi, i, 0)),
        out_shape=jax.ShapeDtypeStruct((b, s, BRANCH), BF16),
        scratch_shapes=[
            pltpu.VMEM((bt + CONV_HALO, BRANCH), F32),
            pltpu.VMEM((SUBLANES - 1, bt + CONV_HALO - SUBLANES, BRANCH), F32),
            pltpu.VMEM((bt, BRANCH), F32),
        ],
        compiler_params=_params("parallel", "arbitrary"),
        name="conv_branch",
    )(u, u, u, u, w, row(conv_b), row(ln_g), row(ln_b))


SUB = 16


def _hgrn_kernel(q_ref, f_ref, v_ref, g_ref, lbl_ref, gn_ref, o_ref, state_ref, *, slot):
    @pl.when(pl.program_id(1) == 0)
    def _():
        state_ref[...] = jnp.zeros_like(state_ref)

    c = CHUNK
    lbl = lbl_ref[...]
    e = jnp.exp(lbl - jnp.max(lbl, axis=0, keepdims=True))
    upto = lax.broadcasted_iota(jnp.int32, e.shape, 0) <= slot
    lb = jnp.sum(jnp.where(upto, e, 0.0), axis=0, keepdims=True) / jnp.sum(e, axis=0, keepdims=True)
    f = lb + (1.0 - lb) * _sigmoid(f_ref[...])
    lf = jnp.log(f)
    kk_all = 1.0 - f
    r_i = lax.broadcasted_iota(jnp.int32, (c, c), 0)
    c_i = lax.broadcasted_iota(jnp.int32, (c, c), 1)
    tri = jnp.where(c_i <= r_i, 1.0, 0.0).astype(BF16)
    hi, mid, lo = _split3(lf)
    l_all = (jnp.dot(tri, hi, preferred_element_type=F32)
             + jnp.dot(tri, mid, preferred_element_type=F32)
             + jnp.dot(tri, lo, preferred_element_type=F32))

    for h in range(HEADS):
        hsl = slice(h * HEAD_DIM, (h + 1) * HEAD_DIM)
        q = q_ref[:, hsl]
        qs = q * _sigmoid(q)
        kk = kk_all[:, hsl]
        ll = l_all[:, hsl]
        v = v_ref[:, hsl].astype(BF16)
        l_end = ll[c - 1:c, :]
        blocks = []
        for tb in range(c // SUB):
            r0 = tb * SUB
            nk = r0 + SUB
            ref = ll[r0 + SUB // 2 - 1:r0 + SUB // 2, :]
            a = qs[r0:nk] * jnp.exp(ll[r0:nk] - ref)
            bm = kk[:nk] * jnp.exp(ref - ll[:nk])
            sc = lax.dot_general(a.astype(BF16), bm.astype(BF16), NT_DIMS, preferred_element_type=F32)
            causal = (lax.broadcasted_iota(jnp.int32, (SUB, nk), 1)
                      <= lax.broadcasted_iota(jnp.int32, (SUB, nk), 0) + r0)
            sc = jnp.where(causal, sc, 0.0)
            blocks.append(jnp.dot(sc.astype(BF16), v[:nk], preferred_element_type=F32))
        o = jnp.concatenate(blocks, axis=0)
        st = state_ref[h]
        o = o + lax.dot_general((qs * jnp.exp(ll)).astype(BF16), st.astype(BF16), NT_DIMS,
                                preferred_element_type=F32)
        kd = (kk * jnp.exp(l_end - ll)).astype(BF16)
        state_ref[h] = jnp.exp(l_end) * st + lax.dot_general(v, kd, TN_DIMS, preferred_element_type=F32)
        o = o * lax.rsqrt(jnp.mean(o * o, axis=-1, keepdims=True) + LN_EPS)
        g = g_ref[:, hsl]
        o_ref[:, hsl] = (o * gn_ref[:, hsl] * (g * _sigmoid(g))).astype(o_ref.dtype)


def hgrn2(u, lb_logits, slot, gnorm_g):
    b, s, _ = u.shape
    slots = lb_logits.shape[0]
    col = lambda j: pl.BlockSpec((None, CHUNK, BRANCH), lambda bi, i: (bi, i, j))
    vec = pl.BlockSpec((1, BRANCH), lambda bi, i: (0, 0))
    return pl.pallas_call(
        functools.partial(_hgrn_kernel, slot=slot),
        grid=(b, s // CHUNK),
        in_specs=[col(2), col(3), col(4), col(5), pl.BlockSpec((slots, BRANCH), lambda bi, i: (0, 0)), vec],
        out_specs=pl.BlockSpec((None, CHUNK, BRANCH), lambda bi, i: (bi, i, 0)),
        out_shape=jax.ShapeDtypeStruct((b, s, BRANCH), BF16),
        scratch_shapes=[pltpu.VMEM((HEADS, HEAD_DIM, HEAD_DIM), F32)],
        compiler_params=_params("parallel", "arbitrary"),
        name="hgrn2",
    )(u, u, u, u, lb_logits.astype(F32), gnorm_g.reshape(1, BRANCH))


FG_ROWS = 16


def _fgate_kernel(x_ref, wt_ref, b_ref, o_ref, carry_ref, *, bm):
    @pl.when(pl.program_id(1) == 0)
    def _():
        carry_ref[...] = jnp.zeros_like(carry_ref)

    z = lax.dot_general(wt_ref[...], x_ref[...].astype(BF16), NT_DIMS, preferred_element_type=F32)
    z = z + b_ref[...]
    ls = jnp.minimum(z, 0.0) - jnp.log(1.0 + jnp.exp(-jnp.abs(z)))
    r_i = lax.broadcasted_iota(jnp.int32, (bm, bm), 0)
    c_i = lax.broadcasted_iota(jnp.int32, (bm, bm), 1)
    tri = jnp.where(r_i <= c_i, 1.0, 0.0).astype(BF16)
    hi, mid, lo = _split3(ls)
    cum = (jnp.dot(hi, tri, preferred_element_type=F32)
           + jnp.dot(mid, tri, preferred_element_type=F32)
           + jnp.dot(lo, tri, preferred_element_type=F32)) + carry_ref[...]
    o_ref[...] = cum
    carry_ref[...] = cum[:, bm - 1:bm]


def forget_cumsum(x, w_f, b_f, bm=512):
    b, s, d = x.shape
    wt = jnp.pad(w_f.T, ((0, FG_ROWS - HEADS), (0, 0))).astype(BF16)
    bias = jnp.pad(b_f, (0, FG_ROWS - HEADS)).reshape(FG_ROWS, 1).astype(F32)
    return pl.pallas_call(
        functools.partial(_fgate_kernel, bm=bm),
        grid=(b, s // bm),
        in_specs=[
            pl.BlockSpec((None, bm, d), lambda bi, i: (bi, i, 0)),
            pl.BlockSpec((FG_ROWS, d), lambda bi, i: (0, 0)),
            pl.BlockSpec((FG_ROWS, 1), lambda bi, i: (0, 0)),
        ],
        out_specs=pl.BlockSpec((None, FG_ROWS, bm), lambda bi, i: (bi, 0, i)),
        out_shape=jax.ShapeDtypeStruct((b, FG_ROWS, s), F32),
        scratch_shapes=[pltpu.VMEM((FG_ROWS, 1), F32)],
        compiler_params=_params("parallel", "arbitrary"),
        name="forget_cumsum",
    )(x, wt, bias)


def _fox_kernel(q_ref, k_ref, v_ref, f_ref, o_ref, *, bq):
    qi = pl.program_id(2)
    q = q_ref[...]

    def step(kb, carry, diagonal):
        m, l, acc = carry
        ks = pl.multiple_of(kb * bq, bq)
        k = k_ref[pl.ds(ks, bq), :]
        v = v_ref[pl.ds(ks, bq), :]
        s = lax.dot_general(q, k, NT_DIMS, preferred_element_type=F32) - f_ref[:, pl.ds(ks, bq)]
        if diagonal:
            r_i = lax.broadcasted_iota(jnp.int32, (bq, bq), 0)
            c_i = lax.broadcasted_iota(jnp.int32, (bq, bq), 1)
            s = jnp.where(c_i <= r_i, s, NEG)
        m_new = jnp.maximum(m, jnp.max(s, axis=1, keepdims=True))
        alpha = jnp.exp(m - m_new)
        p = jnp.exp(s - m_new)
        l = alpha * l + jnp.sum(p, axis=1, keepdims=True)
        acc = alpha * acc + jnp.dot(p.astype(BF16), v, preferred_element_type=F32)
        return m_new, l, acc

    init = (jnp.full((bq, 1), NEG, F32), jnp.zeros((bq, 1), F32), jnp.zeros((bq, HEAD_DIM), F32))
    carry = lax.fori_loop(0, qi, lambda kb, cr: step(kb, cr, False), init)
    _, l, acc = step(qi, carry, True)
    o_ref[...] = (acc / l).astype(o_ref.dtype)


def fox_attention(qkv, fcum, q_col, k_col, v_col, bq=512):
    b, s, _ = qkv.shape
    return pl.pallas_call(
        functools.partial(_fox_kernel, bq=bq),
        grid=(b, HEADS, s // bq),
        in_specs=[
            pl.BlockSpec((None, bq, HEAD_DIM), lambda bi, h, i: (bi, i, q_col + h)),
            pl.BlockSpec((None, s, HEAD_DIM), lambda bi, h, i: (bi, 0, k_col + h)),
            pl.BlockSpec((None, s, HEAD_DIM), lambda bi, h, i: (bi, 0, v_col + h)),
            pl.BlockSpec((None, None, 1, s), lambda bi, h, i: (bi, h, 0, 0)),
        ],
        out_specs=pl.BlockSpec((None, bq, HEAD_DIM), lambda bi, h, i: (bi, i, h)),
        out_shape=jax.ShapeDtypeStruct((b, s, BRANCH), BF16),
        compiler_params=_params("parallel", "parallel", "arbitrary"),
        name="fox_attention",
    )(qkv, qkv, qkv, fcum)


def _chunk_attn_kernel(q_ref, kp_ref, kc_ref, vp_ref, vc_ref, bias_ref, o_ref, *, bq):
    qi = pl.program_id(2)
    q = q_ref[...]
    s0 = lax.dot_general(q, kp_ref[...], NT_DIMS, preferred_element_type=F32) + bias_ref[:, :bq]
    s1 = lax.dot_general(q, kc_ref[...], NT_DIMS, preferred_element_type=F32) + bias_ref[:, bq:]
    s0 = jnp.where(qi > 0, s0, NEG)
    m = jnp.maximum(jnp.max(s0, axis=1, keepdims=True), jnp.max(s1, axis=1, keepdims=True))
    p0 = jnp.exp(s0 - m)
    p1 = jnp.exp(s1 - m)
    l = jnp.sum(p0, axis=1, keepdims=True) + jnp.sum(p1, axis=1, keepdims=True)
    o = (jnp.dot(p0.astype(BF16), vp_ref[...], preferred_element_type=F32)
         + jnp.dot(p1.astype(BF16), vc_ref[...], preferred_element_type=F32))
    o_ref[...] = (o / l).astype(o_ref.dtype)


def _band_bias(rel_bias, bq):
    i = jnp.arange(bq)[:, None]
    j = jnp.arange(2 * bq)[None, :]
    rel = i - (j - bq)
    qc = i // CHUNK
    kc = j // CHUNK - bq // CHUNK
    valid = (kc <= qc) & (kc >= qc - CA_LEFT_CHUNKS)
    idx = jnp.clip(jnp.minimum(rel, REL_CLIP) + (CHUNK - 1), 0, rel_bias.shape[1] - 1)
    return jnp.where(valid[None], rel_bias[:, idx].astype(F32), NEG)


def chunk_attention(qkv, rel_bias, q_col, k_col, v_col, bq=512):
    b, s, _ = qkv.shape
    assert bq >= CA_LEFT_CHUNKS * CHUNK
    bias = _band_bias(rel_bias, bq)
    cur = lambda col: pl.BlockSpec((None, bq, HEAD_DIM), lambda h, bi, i: (bi, i, col + h))
    prev = lambda col: pl.BlockSpec((None, bq, HEAD_DIM), lambda h, bi, i: (bi, jnp.maximum(i - 1, 0), col + h))
    return pl.pallas_call(
        functools.partial(_chunk_attn_kernel, bq=bq),
        grid=(HEADS, b, s // bq),
        in_specs=[cur(q_col), prev(k_col), cur(k_col), prev(v_col), cur(v_col),
                  pl.BlockSpec((None, bq, 2 * bq), lambda h, bi, i: (h, 0, 0))],
        out_specs=pl.BlockSpec((None, bq, HEAD_DIM), lambda h, bi, i: (bi, i, h)),
        out_shape=jax.ShapeDtypeStruct((b, s, BRANCH), BF16),
        compiler_params=_params("parallel", "parallel", "arbitrary"),
        name="chunk_attention",
    )(qkv, qkv, qkv, qkv, qkv, bias)


def _outproj_ln_kernel(a_ref, b_ref, wa_ref, wb_ref, x_ref, g_ref, beta_ref, o_ref):
    mix = (jnp.dot(a_ref[...], wa_ref[...], preferred_element_type=F32)
           + jnp.dot(b_ref[...], wb_ref[...], preferred_element_type=F32))
    o_ref[...] = _layer_norm_rows(ALPHA * x_ref[...] + mix, g_ref[...], beta_ref[...])


def outproj_ln(a, b, w_out, x, g, beta, bm=512):
    t, d = x.shape
    wa = w_out[:BRANCH].astype(BF16)
    wb = w_out[BRANCH:].astype(BF16)
    vec = pl.BlockSpec((1, d), lambda i: (0, 0))
    wspec = pl.BlockSpec((BRANCH, d), lambda i: (0, 0))
    act = pl.BlockSpec((bm, BRANCH), lambda i: (i, 0))
    return pl.pallas_call(
        _outproj_ln_kernel,
        grid=(t // bm,),
        in_specs=[act, act, wspec, wspec, pl.BlockSpec((bm, d), lambda i: (i, 0)), vec, vec],
        out_specs=pl.BlockSpec((bm, d), lambda i: (i, 0)),
        out_shape=jax.ShapeDtypeStruct((t, d), F32),
        compiler_params=_params("parallel"),
        name="outproj_ln",
    )(a, b, wa, wb, x, g.reshape(1, d), beta.reshape(1, d))


def _mlp_ln_kernel(x_ref, w1_ref, w2_ref, g_ref, beta_ref, o_ref, xb_ref, acc_ref):
    j = pl.program_id(1)

    @pl.when(j == 0)
    def _():
        xb_ref[...] = x_ref[...].astype(BF16)
        acc_ref[...] = jnp.zeros_like(acc_ref)

    h = jnp.maximum(jnp.dot(xb_ref[...], w1_ref[...], preferred_element_type=F32), 0.0)
    acc_ref[...] += jnp.dot((h * h).astype(BF16), w2_ref[...], preferred_element_type=F32)

    @pl.when(j == pl.num_programs(1) - 1)
    def _():
        o_ref[...] = _layer_norm_rows(ALPHA * x_ref[...] + acc_ref[...], g_ref[...], beta_ref[...])


def mlp_ln(x, w1, w2, g, beta, bm=512, bf=1024):
    t, d = x.shape
    ff = w1.shape[1]
    vec = pl.BlockSpec((1, d), lambda i, j: (0, 0))
    return pl.pallas_call(
        _mlp_ln_kernel,
        grid=(t // bm, ff // bf),
        in_specs=[
            pl.BlockSpec((bm, d), lambda i, j: (i, 0)),
            pl.BlockSpec((d, bf), lambda i, j: (0, j)),
            pl.BlockSpec((bf, d), lambda i, j: (j, 0)),
            vec, vec,
        ],
        out_specs=pl.BlockSpec((bm, d), lambda i, j: (i, 0)),
        out_shape=jax.ShapeDtypeStruct((t, d), F32),
        scratch_shapes=[pltpu.VMEM((bm, d), BF16), pltpu.VMEM((bm, d), F32)],
        compiler_params=_params("parallel", "arbitrary"),
        name="mlp_ln",
    )(x, w1.astype(BF16), w2.astype(BF16), g.reshape(1, d), beta.reshape(1, d))


def even_mixer(x, w_in, conv_w, conv_b, conv_ln_g, conv_ln_b, lb_logits, slot, gnorm_g):
    b, s, d = x.shape
    n = w_in.shape[1]
    u = inproj(x.reshape(b * s, d), w_in.astype(BF16), jnp.ones((1, n), F32), F32).reshape(b, s, n)
    a_out = conv_branch(u, conv_w, conv_b, conv_ln_g, conv_ln_b)
    b_out = hgrn2(u, lb_logits, slot, gnorm_g)
    return a_out.reshape(b * s, BRANCH), b_out.reshape(b * s, BRANCH)


def odd_mixer(x, w_in, b_f, rel_bias):
    b, s, d = x.shape
    w_qkv = jnp.concatenate([w_in[:, :3 * BRANCH], w_in[:, 3 * BRANCH + HEADS:]], axis=1).astype(BF16)
    w_f = w_in[:, 3 * BRANCH:3 * BRANCH + HEADS]
    scale = HEAD_DIM ** -0.5
    ones = jnp.ones((BRANCH,), F32)
    colscale = jnp.concatenate([ones * scale, ones, ones, ones * scale, ones, ones]).reshape(1, 6 * BRANCH)
    qkv = inproj(x.reshape(b * s, d), w_qkv, colscale, BF16).reshape(b, s, 6 * BRANCH)
    fcum = forget_cumsum(x, w_f, b_f).reshape(b, FG_ROWS, 1, s)
    c_out = fox_attention(qkv, fcum, 0, HEADS, 2 * HEADS)
    d_out = chunk_attention(qkv, rel_bias, 3 * HEADS, 4 * HEADS, 5 * HEADS)
    return c_out.reshape(b * s, BRANCH), d_out.reshape(b * s, BRANCH)


def kernel(x, ev_w_in, ev_conv_w, ev_conv_b, ev_conv_ln_g, ev_conv_ln_b, hgrn_lb_logits, ev_gnorm_g, ev_w_out, od_w_in, fox_b_f, rel_bias, od_w_out, ln_mix_g, ln_mix_b, mlp_w1, mlp_w2, ln_mlp_g, ln_mlp_b):
    b, s, d = x.shape
    for l in range(DEPTH):
        j = l // 2
        if l % 2 == 0:
            p, r = even_mixer(x, ev_w_in[j], ev_conv_w[j], ev_conv_b[j], ev_conv_ln_g[j], ev_conv_ln_b[j],
                              hgrn_lb_logits, j, ev_gnorm_g[j])
            w_out = ev_w_out[j]
        else:
            p, r = odd_mixer(x, od_w_in[j], fox_b_f[j], rel_bias[j])
            w_out = od_w_out[j]
        x2 = outproj_ln(p, r, w_out, x.reshape(b * s, d), ln_mix_g[l], ln_mix_b[l])
        x2 = mlp_ln(x2, mlp_w1[l], mlp_w2[l], ln_mlp_g[l], ln_mlp_b[l])
        x = x2.reshape(b, s, d)
    return x
```

```python
import functools

import jax
import jax.numpy as jnp
from jax import lax
from jax.experimental import pallas as pl
from jax.experimental.pallas import tpu as pltpu

F32 = jnp.float32
BF16 = jnp.bfloat16

D_MODEL = 2048
DEPTH = 2
CHUNK = 64
LN_EPS = 1e-5
ALPHA = (2 * DEPTH) ** 0.25
HEADS = 8
HEAD_DIM = 128
BRANCH = HEADS * HEAD_DIM
CONV_WIDTH = 31
CONV_HALO = 32
CA_LEFT_CHUNKS = 8
REL_CLIP = 256
D_FF = 4 * D_MODEL
NEG = -1e30

LANES = 128
SUBLANES = 8
VMEM_LIMIT = 56 * 1024 * 1024

NT_DIMS = (((1,), (1,)), ((), ()))
TN_DIMS = (((0,), (0,)), ((), ()))


def _params(*sem):
    return pltpu.CompilerParams(dimension_semantics=sem, vmem_limit_bytes=VMEM_LIMIT)


def _sigmoid(x):
    return 1.0 / (1.0 + jnp.exp(-x))


def _layer_norm_rows(y, g, b):
    mu = jnp.mean(y, axis=-1, keepdims=True)
    d = y - mu
    var = jnp.mean(d * d, axis=-1, keepdims=True)
    return d * lax.rsqrt(var + LN_EPS) * g + b


def _split3(x):
    hi = x.astype(BF16)
    r = x - hi.astype(F32)
    mid = r.astype(BF16)
    lo = (r - mid.astype(F32)).astype(BF16)
    return hi, mid, lo


def _inproj_kernel(x_ref, w_ref, s_ref, o_ref, xb_ref):
    @pl.when(pl.program_id(1) == 0)
    def _():
        xb_ref[...] = x_ref[...].astype(BF16)

    acc = jnp.dot(xb_ref[...], w_ref[...], preferred_element_type=F32)
    o_ref[...] = (acc * s_ref[...]).astype(o_ref.dtype)


def inproj(x, w, colscale, out_dtype, bm=1024, bn=1024):
    t, k = x.shape
    n = w.shape[1]
    return pl.pallas_call(
        _inproj_kernel,
        grid=(t // bm, n // bn),
        in_specs=[
            pl.BlockSpec((bm, k), lambda i, j: (i, 0)),
            pl.BlockSpec((k, bn), lambda i, j: (0, j)),
            pl.BlockSpec((1, bn), lambda i, j: (0, j)),
        ],
        out_specs=pl.BlockSpec((bm, bn), lambda i, j: (i, j)),
        out_shape=jax.ShapeDtypeStruct((t, n), out_dtype),
        scratch_shapes=[pltpu.VMEM((bm, k), BF16)],
        compiler_params=_params("parallel", "arbitrary"),
        name="inproj",
    )(x, w, colscale)


CONV_ROWS = 32


def _conv_kernel(a_ref, g_ref, ah_ref, gh_ref, w_ref, cb_ref, lg_ref, lb_ref, o_ref, hext_ref, hs_ref, y_ref, *, bt):
    i = pl.program_id(1)
    halo = ah_ref[...] * _sigmoid(gh_ref[...])
    hext_ref[0:CONV_HALO, :] = jnp.where(i > 0, halo, 0.0)
    hext_ref[CONV_HALO:CONV_HALO + bt, :] = a_ref[...] * _sigmoid(g_ref[...])
    span = bt + CONV_HALO - SUBLANES
    for d in range(1, SUBLANES):
        hs_ref[d - 1, 0:span, :] = hext_ref[d:d + span, :]

    first = CONV_HALO - (CONV_WIDTH - 1)

    def rows(c, carry):
        base = pl.multiple_of(c * CONV_ROWS, CONV_ROWS)
        for cg in range(BRANCH // 256):
            cs = slice(cg * 256, (cg + 1) * 256)
            acc = jnp.zeros((CONV_ROWS, 256), F32) + cb_ref[:, cs]
            for j in range(CONV_WIDTH):
                off = first + j
                d, al = off % SUBLANES, off - off % SUBLANES
                if d == 0:
                    tap = hext_ref[pl.ds(base + al, CONV_ROWS), cs]
                else:
                    tap = hs_ref[d - 1, pl.ds(base + al, CONV_ROWS), cs]
                acc = acc + w_ref[j:j + 1, cs] * tap
            y_ref[pl.ds(base, CONV_ROWS), cs] = acc
        return carry

    lax.fori_loop(0, bt // CONV_ROWS, rows, 0)
    z = _layer_norm_rows(y_ref[...], lg_ref[...], lb_ref[...])
    o_ref[...] = (z * _sigmoid(z)).astype(o_ref.dtype)


def conv_branch(u, conv_w, conv_b, ln_g, ln_b, bt=512):
    b, s, _ = u.shape
    hb = bt // CONV_HALO
    w = jnp.pad(conv_w, ((0, CONV_HALO - CONV_WIDTH), (0, 0)))
    row = lambda v: v.reshape(1, BRANCH)
    vec = pl.BlockSpec((1, BRANCH), lambda bi, i: (0, 0))
    return pl.pallas_call(
        functools.partial(_conv_kernel, bt=bt),
        grid=(b, s // bt),
        in_specs=[
            pl.BlockSpec((None, bt, BRANCH), lambda bi, i: (bi, i, 0)),
            pl.BlockSpec((None, bt, BRANCH), lambda bi, i: (bi, i, 1)),
            pl.BlockSpec((None, CONV_HALO, BRANCH), lambda bi, i: (bi, jnp.maximum(i * hb - 1, 0), 0)),
            pl.BlockSpec((None, CONV_HALO, BRANCH), lambda bi, i: (bi, jnp.maximum(i * hb - 1, 0), 1)),
            pl.BlockSpec((CONV_HALO, BRANCH), lambda bi, i: (0, 0)),
            vec, vec, vec,
        ],
        out_specs=pl.BlockSpec((None, bt, BRANCH), lambda bi, i: (bi, i, 0)),
        out_shape=jax.ShapeDtypeStruct((b, s, BRANCH), BF16),
        scratch_shapes=[
            pltpu.VMEM((bt + CONV_HALO, BRANCH), F32),
            pltpu.VMEM((SUBLANES - 1, bt + CONV_HALO - SUBLANES, BRANCH), F32),
            pltpu.VMEM((bt, BRANCH), F32),
        ],
        compiler_params=_params("parallel", "arbitrary"),
        name="conv_branch",
    )(u, u, u, u, w, row(conv_b), row(ln_g), row(ln_b))


SUB = 16


def _hgrn_kernel(q_ref, f_ref, v_ref, g_ref, lbl_ref, gn_ref, o_ref, state_ref, *, slot):
    @pl.when(pl.program_id(1) == 0)
    def _():
        state_ref[...] = jnp.zeros_like(state_ref)

    c = CHUNK
    lbl = lbl_ref[...]
    e = jnp.exp(lbl - jnp.max(lbl, axis=0, keepdims=True))
    upto = lax.broadcasted_iota(jnp.int32, e.shape, 0) <= slot
    lb = jnp.sum(jnp.where(upto, e, 0.0), axis=0, keepdims=True) / jnp.sum(e, axis=0, keepdims=True)
    f = lb + (1.0 - lb) * _sigmoid(f_ref[...])
    lf = jnp.log(f)
    kk_all = 1.0 - f
    r_i = lax.broadcasted_iota(jnp.int32, (c, c), 0)
    c_i = lax.broadcasted_iota(jnp.int32, (c, c), 1)
    tri = jnp.where(c_i <= r_i, 1.0, 0.0).astype(BF16)
    hi, mid, lo = _split3(lf)
    l_all = (jnp.dot(tri, hi, preferred_element_type=F32)
             + jnp.dot(tri, mid, preferred_element_type=F32)
             + jnp.dot(tri, lo, preferred_element_type=F32))

    for h in range(HEADS):
        hsl = slice(h * HEAD_DIM, (h + 1) * HEAD_DIM)
        q = q_ref[:, hsl]
        qs = q * _sigmoid(q)
        kk = kk_all[:, hsl]
        ll = l_all[:, hsl]
        v = v_ref[:, hsl].astype(BF16)
        l_end = ll[c - 1:c, :]
        blocks = []
        for tb in range(c // SUB):
            r0 = tb * SUB
            nk = r0 + SUB
            ref = ll[r0 + SUB // 2 - 1:r0 + SUB // 2, :]
            a = qs[r0:nk] * jnp.exp(ll[r0:nk] - ref)
            bm = kk[:nk] * jnp.exp(ref - ll[:nk])
            sc = lax.dot_general(a.astype(BF16), bm.astype(BF16), NT_DIMS, preferred_element_type=F32)
            causal = (lax.broadcasted_iota(jnp.int32, (SUB, nk), 1)
                      <= lax.broadcasted_iota(jnp.int32, (SUB, nk), 0) + r0)
            sc = jnp.where(causal, sc, 0.0)
            blocks.append(jnp.dot(sc.astype(BF16), v[:nk], preferred_element_type=F32))
        o = jnp.concatenate(blocks, axis=0)
        st = state_ref[h]
        o = o + lax.dot_general((qs * jnp.exp(ll)).astype(BF16), st.astype(BF16), NT_DIMS,
                                preferred_element_type=F32)
        kd = (kk * jnp.exp(l_end - ll)).astype(BF16)
        state_ref[h] = jnp.exp(l_end) * st + lax.dot_general(v, kd, TN_DIMS, preferred_element_type=F32)
        o = o * lax.rsqrt(jnp.mean(o * o, axis=-1, keepdims=True) + LN_EPS)
        g = g_ref[:, hsl]
        o_ref[:, hsl] = (o * gn_ref[:, hsl] * (g * _sigmoid(g))).astype(o_ref.dtype)


def hgrn2(u, lb_logits, slot, gnorm_g):
    b, s, _ = u.shape
    slots = lb_logits.shape[0]
    col = lambda j: pl.BlockSpec((None, CHUNK, BRANCH), lambda bi, i: (bi, i, j))
    vec = pl.BlockSpec((1, BRANCH), lambda bi, i: (0, 0))
    return pl.pallas_call(
        functools.partial(_hgrn_kernel, slot=slot),
        grid=(b, s // CHUNK),
        in_specs=[col(2), col(3), col(4), col(5), pl.BlockSpec((slots, BRANCH), lambda bi, i: (0, 0)), vec],
        out_specs=pl.BlockSpec((None, CHUNK, BRANCH), lambda bi, i: (bi, i, 0)),
        out_shape=jax.ShapeDtypeStruct((b, s, BRANCH), BF16),
        scratch_shapes=[pltpu.VMEM((HEADS, HEAD_DIM, HEAD_DIM), F32)],
        compiler_params=_params("parallel", "arbitrary"),
        name="hgrn2",
    )(u, u, u, u, lb_logits.astype(F32), gnorm_g.reshape(1, BRANCH))


def _fgate_kernel(x_ref, w_ref, b_ref, o_ref, carry_ref, *, bm):
    @pl.when(pl.program_id(1) == 0)
    def _():
        carry_ref[...] = jnp.zeros_like(carry_ref)

    z = jnp.dot(x_ref[...].astype(BF16), w_ref[...], preferred_element_type=F32) + b_ref[...]
    ls = jnp.minimum(z, 0.0) - jnp.log(1.0 + jnp.exp(-jnp.abs(z)))
    r_i = lax.broadcasted_iota(jnp.int32, (bm, bm), 0)
    c_i = lax.broadcasted_iota(jnp.int32, (bm, bm), 1)
    tri = jnp.where(c_i <= r_i, 1.0, 0.0).astype(BF16)
    hi, mid, lo = _split3(ls)
    cum = (jnp.dot(tri, hi, preferred_element_type=F32)
           + jnp.dot(tri, mid, preferred_element_type=F32)
           + jnp.dot(tri, lo, preferred_element_type=F32)) + carry_ref[...]
    carry_ref[...] = cum[bm - 1:bm, :]
    for h in range(HEADS):
        o_ref[h] = jnp.broadcast_to(cum[:, h:h + 1], (bm, LANES))


def forget_cumsum(x, w_f, b_f, bm=512):
    b, s, d = x.shape
    w = jnp.pad(w_f, ((0, 0), (0, LANES - HEADS))).astype(BF16)
    bias = jnp.pad(b_f, (0, LANES - HEADS)).reshape(1, LANES).astype(F32)
    return pl.pallas_call(
        functools.partial(_fgate_kernel, bm=bm),
        grid=(b, s // bm),
        in_specs=[
            pl.BlockSpec((None, bm, d), lambda bi, i: (bi, i, 0)),
            pl.BlockSpec((d, LANES), lambda bi, i: (0, 0)),
            pl.BlockSpec((1, LANES), lambda bi, i: (0, 0)),
        ],
        out_specs=pl.BlockSpec((None, HEADS, bm, LANES), lambda bi, i: (bi, 0, i, 0)),
        out_shape=jax.ShapeDtypeStruct((b, HEADS, s, LANES), F32),
        scratch_shapes=[pltpu.VMEM((1, LANES), F32)],
        compiler_params=_params("parallel", "arbitrary"),
        name="forget_cumsum",
    )(x, w, bias)


FOX_KSPLIT = 1


def _fox_kernel(q_ref, k_ref, v_ref, f_ref, o_ref, vt_ref, acc_ref, *, bq):
    qi = pl.program_id(2)
    nblk = pl.num_programs(2)

    @pl.when(qi == 0)
    def _():
        def tr(c, carry):
            cs = pl.multiple_of(c * bq, bq)
            vt_ref[:, pl.ds(cs, bq)] = v_ref[pl.ds(cs, bq), :].astype(F32).T.astype(BF16)
            return carry
        lax.fori_loop(0, nblk, tr, 0)

    q = q_ref[...]
    acc_ref[...] = jnp.zeros_like(acc_ref)
    bk = bq // FOX_KSPLIT

    def step(kb, carry, diagonal):
        m, l = carry
        for u in range(FOX_KSPLIT):
            ks = pl.multiple_of(kb * bq + u * bk, bk)
            st = lax.dot_general(k_ref[pl.ds(ks, bk), :], q, NT_DIMS, preferred_element_type=F32)
            st = st - pltpu.repeat(f_ref[pl.ds(ks, bk), :], bq // LANES, axis=1)
            if diagonal:
                r_i = lax.broadcasted_iota(jnp.int32, (bk, bq), 0) + u * bk
                c_i = lax.broadcasted_iota(jnp.int32, (bk, bq), 1)
                st = jnp.where(r_i <= c_i, st, NEG)
            m_new = jnp.maximum(m, jnp.max(st, axis=0, keepdims=True))
            alpha = jnp.exp(m - m_new)
            p = jnp.exp(st - m_new)
            l = alpha * l + jnp.sum(p, axis=0, keepdims=True)
            acc_ref[...] = alpha * acc_ref[...] + jnp.dot(vt_ref[:, pl.ds(ks, bk)], p.astype(BF16),
                                                          preferred_element_type=F32)
            m = m_new
        return m, l

    init = (jnp.full((1, bq), NEG, F32), jnp.zeros((1, bq), F32))
    carry = lax.fori_loop(0, qi, lambda kb, cr: step(kb, cr, False), init)
    _, l = step(qi, carry, True)
    o_ref[...] = (acc_ref[...] / l).T.astype(o_ref.dtype)


def fox_attention(qkv, fcum, q_col, k_col, v_col, bq=512):
    b, s, _ = qkv.shape
    return pl.pallas_call(
        functools.partial(_fox_kernel, bq=bq),
        grid=(b, HEADS, s // bq),
        in_specs=[
            pl.BlockSpec((None, bq, HEAD_DIM), lambda bi, h, i: (bi, i, q_col + h)),
            pl.BlockSpec((None, s, HEAD_DIM), lambda bi, h, i: (bi, 0, k_col + h)),
            pl.BlockSpec((None, s, HEAD_DIM), lambda bi, h, i: (bi, 0, v_col + h)),
            pl.BlockSpec((None, None, s, LANES), lambda bi, h, i: (bi, h, 0, 0)),
        ],
        out_specs=pl.BlockSpec((None, bq, HEAD_DIM), lambda bi, h, i: (bi, i, h)),
        out_shape=jax.ShapeDtypeStruct((b, s, BRANCH), BF16),
        scratch_shapes=[pltpu.VMEM((HEAD_DIM, s), BF16), pltpu.VMEM((HEAD_DIM, bq), F32)],
        compiler_params=_params("parallel", "parallel", "arbitrary"),
        name="fox_attention",
    )(qkv, qkv, qkv, fcum)


def _chunk_attn_kernel(q_ref, kp_ref, kc_ref, vp_ref, vc_ref, row_ref, o_ref, bias_ref, *, bq):
    qi = pl.program_id(2)

    @pl.when((pl.program_id(1) == 0) & (qi == 0))
    def _():
        w = 2 * bq
        toeplitz = pltpu.roll(jnp.broadcast_to(row_ref[...], (bq, w)), 0, 1, stride=1, stride_axis=0)
        qc = lax.broadcasted_iota(jnp.int32, (bq, w), 0) // CHUNK
        kc = lax.broadcasted_iota(jnp.int32, (bq, w), 1) // CHUNK - bq // CHUNK
        band = jnp.where(kc <= qc, jnp.where(kc >= qc - CA_LEFT_CHUNKS, toeplitz, NEG), NEG)
        bias_ref[...] = band

    q = q_ref[...]
    s0 = lax.dot_general(q, kp_ref[...], NT_DIMS, preferred_element_type=F32) + bias_ref[:, :bq]
    s1 = lax.dot_general(q, kc_ref[...], NT_DIMS, preferred_element_type=F32) + bias_ref[:, bq:]
    s0 = jnp.where(qi > 0, s0, NEG)
    m = jnp.maximum(jnp.max(s0, axis=1, keepdims=True), jnp.max(s1, axis=1, keepdims=True))
    p0 = jnp.exp(s0 - m)
    p1 = jnp.exp(s1 - m)
    l = jnp.sum(p0, axis=1, keepdims=True) + jnp.sum(p1, axis=1, keepdims=True)
    o = (jnp.dot(p0.astype(BF16), vp_ref[...], preferred_element_type=F32)
         + jnp.dot(p1.astype(BF16), vc_ref[...], preferred_element_type=F32))
    o_ref[...] = (o / l).astype(o_ref.dtype)


def _distance_rows(rel_bias, bq):
    heads, table = rel_bias.shape
    assert table == (CHUNK - 1) + REL_CLIP + 1 and bq >= REL_CLIP
    far = rel_bias[:, table - 1:]
    return jnp.concatenate([
        jnp.broadcast_to(far, (heads, bq - REL_CLIP)),
        rel_bias[:, ::-1],
        jnp.broadcast_to(far, (heads, bq - CHUNK)),
    ], axis=1).astype(F32).reshape(heads, 1, 2 * bq)


def chunk_attention(qkv, rel_bias, q_col, k_col, v_col, bq=512):
    b, s, _ = qkv.shape
    assert bq >= CA_LEFT_CHUNKS * CHUNK
    rows = _distance_rows(rel_bias, bq)
    cur = lambda col: pl.BlockSpec((None, bq, HEAD_DIM), lambda h, bi, i: (bi, i, col + h))
    prev = lambda col: pl.BlockSpec((None, bq, HEAD_DIM), lambda h, bi, i: (bi, jnp.maximum(i - 1, 0), col + h))
    return pl.pallas_call(
        functools.partial(_chunk_attn_kernel, bq=bq),
        grid=(HEADS, b, s // bq),
        in_specs=[cur(q_col), prev(k_col), cur(k_col), prev(v_col), cur(v_col),
                  pl.BlockSpec((None, 1, 2 * bq), lambda h, bi, i: (h, 0, 0))],
        out_specs=pl.BlockSpec((None, bq, HEAD_DIM), lambda h, bi, i: (bi, i, h)),
        out_shape=jax.ShapeDtypeStruct((b, s, BRANCH), BF16),
        scratch_shapes=[pltpu.VMEM((bq, 2 * bq), F32)],
        compiler_params=_params("arbitrary", "arbitrary", "arbitrary"),
        name="chunk_attention",
    )(qkv, qkv, qkv, qkv, qkv, rows)


def _outproj_ln_kernel(a_ref, b_ref, wa_ref, wb_ref, x_ref, g_ref, beta_ref, o_ref):
    mix = (jnp.dot(a_ref[...], wa_ref[...], preferred_element_type=F32)
           + jnp.dot(b_ref[...], wb_ref[...], preferred_element_type=F32))
    o_ref[...] = _layer_norm_rows(ALPHA * x_ref[...] + mix, g_ref[...], beta_ref[...])


def outproj_ln(a, b, w_out, x, g, beta, bm=512):
    t, d = x.shape
    wa = w_out[:BRANCH].astype(BF16)
    wb = w_out[BRANCH:].astype(BF16)
    vec = pl.BlockSpec((1, d), lambda i: (0, 0))
    wspec = pl.BlockSpec((BRANCH, d), lambda i: (0, 0))
    act = pl.BlockSpec((bm, BRANCH), lambda i: (i, 0))
    return pl.pallas_call(
        _outproj_ln_kernel,
        grid=(t // bm,),
        in_specs=[act, act, wspec, wspec, pl.BlockSpec((bm, d), lambda i: (i, 0)), vec, vec],
        out_specs=pl.BlockSpec((bm, d), lambda i: (i, 0)),
        out_shape=jax.ShapeDtypeStruct((t, d), F32),
        compiler_params=_params("parallel"),
        name="outproj_ln",
    )(a, b, wa, wb, x, g.reshape(1, d), beta.reshape(1, d))


def _mlp_ln_kernel(x_ref, w1_ref, w2_ref, g_ref, beta_ref, o_ref, xb_ref, acc_ref):
    j = pl.program_id(1)

    @pl.when(j == 0)
    def _():
        xb_ref[...] = x_ref[...].astype(BF16)
        acc_ref[...] = jnp.zeros_like(acc_ref)

    h = jnp.maximum(jnp.dot(xb_ref[...], w1_ref[...], preferred_element_type=F32), 0.0)
    acc_ref[...] += jnp.dot((h * h).astype(BF16), w2_ref[...], preferred_element_type=F32)

    @pl.when(j == pl.num_programs(1) - 1)
    def _():
        o_ref[...] = _layer_norm_rows(ALPHA * x_ref[...] + acc_ref[...], g_ref[...], beta_ref[...])


def mlp_ln(x, w1, w2, g, beta, bm=512, bf=1024):
    t, d = x.shape
    ff = w1.shape[1]
    vec = pl.BlockSpec((1, d), lambda i, j: (0, 0))
    return pl.pallas_call(
        _mlp_ln_kernel,
        grid=(t // bm, ff // bf),
        in_specs=[
            pl.BlockSpec((bm, d), lambda i, j: (i, 0)),
            pl.BlockSpec((d, bf), lambda i, j: (0, j)),
            pl.BlockSpec((bf, d), lambda i, j: (j, 0)),
            vec, vec,
        ],
        out_specs=pl.BlockSpec((bm, d), lambda i, j: (i, 0)),
        out_shape=jax.ShapeDtypeStruct((t, d), F32),
        scratch_shapes=[pltpu.VMEM((bm, d), BF16), pltpu.VMEM((bm, d), F32)],
        compiler_params=_params("parallel", "arbitrary"),
        name="mlp_ln",
    )(x, w1.astype(BF16), w2.astype(BF16), g.reshape(1, d), beta.reshape(1, d))


def even_mixer(x, w_in, conv_w, conv_b, conv_ln_g, conv_ln_b, lb_logits, slot, gnorm_g):
    b, s, d = x.shape
    n = w_in.shape[1]
    u = inproj(x.reshape(b * s, d), w_in.astype(BF16), jnp.ones((1, n), F32), F32).reshape(b, s, n)
    a_out = conv_branch(u, conv_w, conv_b, conv_ln_g, conv_ln_b)
    b_out = hgrn2(u, lb_logits, slot, gnorm_g)
    return a_out.reshape(b * s, BRANCH), b_out.reshape(b * s, BRANCH)


def odd_mixer(x, w_in, b_f, rel_bias):
    b, s, d = x.shape
    w_qkv = jnp.concatenate([w_in[:, :3 * BRANCH], w_in[:, 3 * BRANCH + HEADS:]], axis=1).astype(BF16)
    w_f = w_in[:, 3 * BRANCH:3 * BRANCH + HEADS]
    scale = HEAD_DIM ** -0.5
    ones = jnp.ones((BRANCH,), F32)
    colscale = jnp.concatenate([ones * scale, ones, ones, ones * scale, ones, ones]).reshape(1, 6 * BRANCH)
    qkv = inproj(x.reshape(b * s, d), w_qkv, colscale, BF16).reshape(b, s, 6 * BRANCH)
    fcum = forget_cumsum(x, w_f, b_f)
    c_out = fox_attention(qkv, fcum, 0, HEADS, 2 * HEADS)
    d_out = chunk_attention(qkv, rel_bias, 3 * HEADS, 4 * HEADS, 5 * HEADS)
    return c_out.reshape(b * s, BRANCH), d_out.reshape(b * s, BRANCH)


def kernel(x, ev_w_in, ev_conv_w, ev_conv_b, ev_conv_ln_g, ev_conv_ln_b, hgrn_lb_logits, ev_gnorm_g, ev_w_out, od_w_in, fox_b_f, rel_bias, od_w_out, ln_mix_g, ln_mix_b, mlp_w1, mlp_w2, ln_mlp_g, ln_mlp_b):
    b, s, d = x.shape
    for l in range(DEPTH):
        j = l // 2
        if l % 2 == 0:
            p, r = even_mixer(x, ev_w_in[j], ev_conv_w[j], ev_conv_b[j], ev_conv_ln_g[j], ev_conv_ln_b[j],
                              hgrn_lb_logits, j, ev_gnorm_g[j])
            w_out = ev_w_out[j]
        else:
            p, r = odd_mixer(x, od_w_in[j], fox_b_f[j], rel_bias[j])
            w_out = od_w_out[j]
        x2 = outproj_ln(p, r, w_out, x.reshape(b * s, d), ln_mix_g[l], ln_mix_b[l])
        x2 = mlp_ln(x2, mlp_w1[l], mlp_w2[l], ln_mlp_g[l], ln_mlp_b[l])
        x = x2.reshape(b, s, d)
    return x
```

```python
import functools

import jax
import jax.numpy as jnp
from jax import lax
from jax.experimental import pallas as pl
from jax.experimental.pallas import tpu as pltpu

F32 = jnp.float32
BF16 = jnp.bfloat16

D_MODEL = 2048
DEPTH = 2
CHUNK = 64
LN_EPS = 1e-5
ALPHA = (2 * DEPTH) ** 0.25
HEADS = 8
HEAD_DIM = 128
BRANCH = HEADS * HEAD_DIM
CONV_WIDTH = 31
CONV_HALO = 32
CA_LEFT_CHUNKS = 8
REL_CLIP = 256
D_FF = 4 * D_MODEL
NEG = -1e30
LOG2E = 1.4426950408889634

LANES = 128
SUBLANES = 8
VMEM_LIMIT = 56 * 1024 * 1024

NT_DIMS = (((1,), (1,)), ((), ()))
TN_DIMS = (((0,), (0,)), ((), ()))


def _params(*sem):
    return pltpu.CompilerParams(dimension_semantics=sem, vmem_limit_bytes=VMEM_LIMIT)


def _sigmoid(x):
    return 1.0 / (1.0 + jnp.exp(-x))


def _layer_norm_rows(y, g, b):
    mu = jnp.mean(y, axis=-1, keepdims=True)
    d = y - mu
    var = jnp.mean(d * d, axis=-1, keepdims=True)
    return d * lax.rsqrt(var + LN_EPS) * g + b


def _split3(x):
    hi = x.astype(BF16)
    r = x - hi.astype(F32)
    mid = r.astype(BF16)
    lo = (r - mid.astype(F32)).astype(BF16)
    return hi, mid, lo


def _inproj_kernel(x_ref, w_ref, s_ref, o_ref, xb_ref):
    @pl.when(pl.program_id(1) == 0)
    def _():
        xb_ref[...] = x_ref[...].astype(BF16)

    acc = jnp.dot(xb_ref[...], w_ref[...], preferred_element_type=F32)
    o_ref[...] = (acc * s_ref[...]).astype(o_ref.dtype)


def inproj(x, w, colscale, out_dtype, bm=1024, bn=1024):
    t, k = x.shape
    n = w.shape[1]
    return pl.pallas_call(
        _inproj_kernel,
        grid=(t // bm, n // bn),
        in_specs=[
            pl.BlockSpec((bm, k), lambda i, j: (i, 0)),
            pl.BlockSpec((k, bn), lambda i, j: (0, j)),
            pl.BlockSpec((1, bn), lambda i, j: (0, j)),
        ],
        out_specs=pl.BlockSpec((bm, bn), lambda i, j: (i, j)),
        out_shape=jax.ShapeDtypeStruct((t, n), out_dtype),
        scratch_shapes=[pltpu.VMEM((bm, k), BF16)],
        compiler_params=_params("parallel", "arbitrary"),
        name="inproj",
    )(x, w, colscale)


CONV_ROWS = 32


def _conv_kernel(a_ref, g_ref, ah_ref, gh_ref, w_ref, cb_ref, lg_ref, lb_ref, o_ref, hext_ref, hs_ref, y_ref, *, bt):
    i = pl.program_id(1)
    halo = ah_ref[...] * _sigmoid(gh_ref[...])
    hext_ref[0:CONV_HALO, :] = jnp.where(i > 0, halo, 0.0)
    hext_ref[CONV_HALO:CONV_HALO + bt, :] = a_ref[...] * _sigmoid(g_ref[...])
    span = bt + CONV_HALO - SUBLANES
    for d in range(1, SUBLANES):
        hs_ref[d - 1, 0:span, :] = hext_ref[d:d + span, :]

    first = CONV_HALO - (CONV_WIDTH - 1)

    def rows(c, carry):
        base = pl.multiple_of(c * CONV_ROWS, CONV_ROWS)
        for cg in range(BRANCH // 256):
            cs = slice(cg * 256, (cg + 1) * 256)
            acc = jnp.zeros((CONV_ROWS, 256), F32) + cb_ref[:, cs]
            for j in range(CONV_WIDTH):
                off = first + j
                d, al = off % SUBLANES, off - off % SUBLANES
                if d == 0:
                    tap = hext_ref[pl.ds(base + al, CONV_ROWS), cs]
                else:
                    tap = hs_ref[d - 1, pl.ds(base + al, CONV_ROWS), cs]
                acc = acc + w_ref[j:j + 1, cs] * tap
            y_ref[pl.ds(base, CONV_ROWS), cs] = acc
        return carry

    lax.fori_loop(0, bt // CONV_ROWS, rows, 0)
    z = _layer_norm_rows(y_ref[...], lg_ref[...], lb_ref[...])
    o_ref[...] = (z * _sigmoid(z)).astype(o_ref.dtype)


def conv_branch(u, conv_w, conv_b, ln_g, ln_b, bt=512):
    b, s, _ = u.shape
    hb = bt // CONV_HALO
    w = jnp.pad(conv_w, ((0, CONV_HALO - CONV_WIDTH), (0, 0)))
    row = lambda v: v.reshape(1, BRANCH)
    vec = pl.BlockSpec((1, BRANCH), lambda bi, i: (0, 0))
    return pl.pallas_call(
        functools.partial(_conv_kernel, bt=bt),
        grid=(b, s // bt),
        in_specs=[
            pl.BlockSpec((None, bt, BRANCH), lambda bi, i: (bi, i, 0)),
            pl.BlockSpec((None, bt, BRANCH), lambda bi, i: (bi, i, 1)),
            pl.BlockSpec((None, CONV_HALO, BRANCH), lambda bi, i: (bi, jnp.maximum(i * hb - 1, 0), 0)),
            pl.BlockSpec((None, CONV_HALO, BRANCH), lambda bi, i: (bi, jnp.maximum(i * hb - 1, 0), 1)),
            pl.BlockSpec((CONV_HALO, BRANCH), lambda bi, i: (0, 0)),
            vec, vec, vec,
        ],
        out_specs=pl.BlockSpec((None, bt, BRANCH), lambda bi, i: (bi, i, 0)),
        out_shape=jax.ShapeDtypeStruct((b, s, BRANCH), BF16),
        scratch_shapes=[
            pltpu.VMEM((bt + CONV_HALO, BRANCH), F32),
            pltpu.VMEM((SUBLANES - 1, bt + CONV_HALO - SUBLANES, BRANCH), F32),
            pltpu.VMEM((bt, BRANCH), F32),
        ],
        compiler_params=_params("parallel", "arbitrary"),
        name="conv_branch",
    )(u, u, u, u, w, row(conv_b), row(ln_g), row(ln_b))


HG_HALF = CHUNK // 2
HG_PAIR = 2 * HEAD_DIM


def _hgrn_kernel(q_ref, f_ref, v_ref, g_ref, lbl_ref, gn_ref, o_ref, state_ref, *, slot):
    @pl.when(pl.program_id(1) == 0)
    def _():
        state_ref[...] = jnp.zeros_like(state_ref)

    c, hh = CHUNK, HG_HALF
    lbl = lbl_ref[...]
    e = jnp.exp(lbl - jnp.max(lbl, axis=0, keepdims=True))
    upto = lax.broadcasted_iota(jnp.int32, e.shape, 0) <= slot
    lb = jnp.sum(jnp.where(upto, e, 0.0), axis=0, keepdims=True) / jnp.sum(e, axis=0, keepdims=True)
    f = lb + (1.0 - lb) * _sigmoid(f_ref[...])
    lf = jnp.log(f)
    kk = 1.0 - f
    q = q_ref[...]
    qs = q * _sigmoid(q)
    r_i = lax.broadcasted_iota(jnp.int32, (c, c), 0)
    c_i = lax.broadcasted_iota(jnp.int32, (c, c), 1)
    tri = jnp.where(c_i <= r_i, 1.0, 0.0).astype(BF16)
    l3 = jnp.dot(tri, jnp.concatenate(_split3(lf), axis=1), preferred_element_type=F32)
    ll = l3[:, :BRANCH] + l3[:, BRANCH:2 * BRANCH] + l3[:, 2 * BRANCH:]
    l_mid0 = ll[hh // 2 - 1:hh // 2]
    l_edge = ll[hh - 1:hh]
    l_mid1 = ll[hh + hh // 2 - 1:hh + hh // 2]
    l_end = ll[c - 1:c]
    lt, lbt = ll[:hh], ll[hh:]
    a_d0 = qs[:hh] * jnp.exp(lt - l_mid0)
    b_d0 = kk[:hh] * jnp.exp(l_mid0 - lt)
    a_off = qs[hh:] * jnp.exp(lbt - l_edge)
    b_off = kk[:hh] * jnp.exp(l_edge - lt)
    a_d1 = qs[hh:] * jnp.exp(lbt - l_mid1)
    b_d1 = kk[hh:] * jnp.exp(l_mid1 - lbt)
    a_int = (qs * jnp.exp(ll)).astype(BF16)
    kd = (kk * jnp.exp(l_end - ll)).astype(BF16)
    dec = jnp.exp(l_end)

    zero = jnp.zeros((hh, HG_PAIR), F32)
    first = lax.broadcasted_iota(jnp.int32, (c, HG_PAIR), 1) < HEAD_DIM
    first3 = jnp.concatenate([first, first, first], axis=1)
    row2 = lax.broadcasted_iota(jnp.int32, (HG_PAIR, HG_PAIR), 0) < HEAD_DIM
    col2 = lax.broadcasted_iota(jnp.int32, (HG_PAIR, HG_PAIR), 1) < HEAD_DIM
    same_head = row2 == col2
    key_i = lax.broadcasted_iota(jnp.int32, (c, 2 * c), 1) % c
    qry_i = lax.broadcasted_iota(jnp.int32, (c, 2 * c), 0)
    stack = lambda top, bottom: jnp.concatenate([top, bottom], axis=0)

    for p in range(HEADS // 2):
        ps = slice(p * HG_PAIR, (p + 1) * HG_PAIR)
        lhs = jnp.concatenate([stack(a_d0[:, ps], zero), stack(zero, a_off[:, ps]), stack(zero, a_d1[:, ps])], axis=1)
        keys = jnp.concatenate([stack(b_d0[:, ps], zero), stack(b_off[:, ps], zero), stack(zero, b_d1[:, ps])], axis=1)
        rhs_t = stack(jnp.where(first3, keys, 0.0), jnp.where(first3, 0.0, keys))
        sc = lax.dot_general(lhs.astype(BF16), rhs_t.astype(BF16), NT_DIMS, preferred_element_type=F32)
        sc = jnp.where(key_i <= qry_i, sc, 0.0)
        v = v_ref[:, ps]
        v_bd = stack(jnp.where(first, v, 0.0), jnp.where(first, 0.0, v)).astype(BF16)
        st = state_ref[p]
        o = (jnp.dot(sc.astype(BF16), v_bd, preferred_element_type=F32)
             + lax.dot_general(a_int[:, ps], st.astype(BF16), NT_DIMS, preferred_element_type=F32))
        upd = lax.dot_general(v.astype(BF16), kd[:, ps], TN_DIMS, preferred_element_type=F32)
        state_ref[p] = dec[:, ps] * st + jnp.where(same_head, upd, 0.0)
        for j in range(2):
            hsl = slice(p * HG_PAIR + j * HEAD_DIM, p * HG_PAIR + (j + 1) * HEAD_DIM)
            oj = o[:, j * HEAD_DIM:(j + 1) * HEAD_DIM]
            oj = oj * lax.rsqrt(jnp.mean(oj * oj, axis=-1, keepdims=True) + LN_EPS)
            g = g_ref[:, hsl]
            o_ref[:, hsl] = (oj * gn_ref[:, hsl] * (g * _sigmoid(g))).astype(o_ref.dtype)


def hgrn2(u, lb_logits, slot, gnorm_g):
    b, s, _ = u.shape
    slots = lb_logits.shape[0]
    col = lambda j: pl.BlockSpec((None, CHUNK, BRANCH), lambda bi, i: (bi, i, j))
    vec = pl.BlockSpec((1, BRANCH), lambda bi, i: (0, 0))
    return pl.pallas_call(
        functools.partial(_hgrn_kernel, slot=slot),
        grid=(b, s // CHUNK),
        in_specs=[col(2), col(3), col(4), col(5), pl.BlockSpec((slots, BRANCH), lambda bi, i: (0, 0)), vec],
        out_specs=pl.BlockSpec((None, CHUNK, BRANCH), lambda bi, i: (bi, i, 0)),
        out_shape=jax.ShapeDtypeStruct((b, s, BRANCH), BF16),
        scratch_shapes=[pltpu.VMEM((HEADS // 2, HG_PAIR, HG_PAIR), F32)],
        compiler_params=_params("parallel", "arbitrary"),
        name="hgrn2",
    )(u, u, u, u, lb_logits.astype(F32), gnorm_g.reshape(1, BRANCH))


def _fgate_kernel(x_ref, w_ref, b_ref, o_ref, carry_ref, *, bm):
    @pl.when(pl.program_id(1) == 0)
    def _():
        carry_ref[...] = jnp.zeros_like(carry_ref)

    z = jnp.dot(x_ref[...].astype(BF16), w_ref[...], preferred_element_type=F32) + b_ref[...]
    ls = jnp.minimum(z, 0.0) - jnp.log(1.0 + jnp.exp(-jnp.abs(z)))
    r_i = lax.broadcasted_iota(jnp.int32, (bm, bm), 0)
    c_i = lax.broadcasted_iota(jnp.int32, (bm, bm), 1)
    tri = jnp.where(c_i <= r_i, 1.0, 0.0).astype(BF16)
    hi, mid, lo = _split3(ls)
    cum = (jnp.dot(tri, hi, preferred_element_type=F32)
           + jnp.dot(tri, mid, preferred_element_type=F32)
           + jnp.dot(tri, lo, preferred_element_type=F32)) + carry_ref[...]
    carry_ref[...] = cum[bm - 1:bm, :]
    cum2 = cum * LOG2E
    for h in range(HEADS):
        o_ref[h] = jnp.broadcast_to(cum2[:, h:h + 1], (bm, LANES))


def forget_cumsum(x, w_f, b_f, bm=512):
    b, s, d = x.shape
    w = jnp.pad(w_f, ((0, 0), (0, LANES - HEADS))).astype(BF16)
    bias = jnp.pad(b_f, (0, LANES - HEADS)).reshape(1, LANES).astype(F32)
    return pl.pallas_call(
        functools.partial(_fgate_kernel, bm=bm),
        grid=(b, s // bm),
        in_specs=[
            pl.BlockSpec((None, bm, d), lambda bi, i: (bi, i, 0)),
            pl.BlockSpec((d, LANES), lambda bi, i: (0, 0)),
            pl.BlockSpec((1, LANES), lambda bi, i: (0, 0)),
        ],
        out_specs=pl.BlockSpec((None, HEADS, bm, LANES), lambda bi, i: (bi, 0, i, 0)),
        out_shape=jax.ShapeDtypeStruct((b, HEADS, s, LANES), F32),
        scratch_shapes=[pltpu.VMEM((1, LANES), F32)],
        compiler_params=_params("parallel", "arbitrary"),
        name="forget_cumsum",
    )(x, w, bias)


FOX_GROUP = 4


def _fox_kernel(q_ref, k_ref, v_ref, f_ref, o_ref, vt_ref, acc_ref, *, bq):
    qi = pl.program_id(2)
    nblk = pl.num_programs(2)
    d = HEAD_DIM
    heads = range(FOX_GROUP)
    hs = [slice(h * d, (h + 1) * d) for h in heads]

    @pl.when(qi == 0)
    def _():
        def tr(c, carry):
            cs = pl.multiple_of(c * bq, bq)
            for h in heads:
                vt_ref[h, :, pl.ds(cs, bq)] = v_ref[pl.ds(cs, bq), hs[h]].astype(F32).T.astype(BF16)
            return carry
        lax.fori_loop(0, nblk, tr, 0)

    acc_ref[...] = jnp.zeros_like(acc_ref)

    def scores(h, kb):
        ks = pl.multiple_of(kb * bq, bq)
        st = lax.dot_general(k_ref[pl.ds(ks, bq), hs[h]], q_ref[:, hs[h]], NT_DIMS, preferred_element_type=F32)
        return st - jnp.tile(f_ref[h, pl.ds(ks, bq), :], (1, bq // LANES))

    def values(h, kb, p):
        ks = pl.multiple_of(kb * bq, bq)
        return jnp.dot(vt_ref[h, :, pl.ds(ks, bq)], p, preferred_element_type=F32)

    def softmax_step(st, m, l):
        m_new = jnp.maximum(m, jnp.max(st, axis=0, keepdims=True))
        alpha = jnp.exp2(m - m_new)
        p = jnp.exp2(st - m_new)
        return m_new, alpha, alpha * l + jnp.sum(p, axis=0, keepdims=True), p.astype(BF16)

    def step(kb, carry, diagonal):
        sts = [scores(h, kb) for h in heads]
        if diagonal:
            keep = (lax.broadcasted_iota(jnp.int32, (bq, bq), 0) <= lax.broadcasted_iota(jnp.int32, (bq, bq), 1))
            sts = [jnp.where(keep, st, NEG) for st in sts]
        out = []
        for h in heads:
            m, l = carry[h]
            m, alpha, l, p = softmax_step(sts[h], m, l)
            acc_ref[h] = alpha * acc_ref[h] + values(h, kb, p)
            out.append((m, l))
        return tuple(out)

    init = tuple((jnp.full((1, bq), NEG, F32), jnp.zeros((1, bq), F32)) for _ in heads)
    carry = lax.fori_loop(0, qi, lambda kb, cr: step(kb, cr, False), init)
    carry = step(qi, carry, True)
    for h in heads:
        o_ref[:, hs[h]] = (acc_ref[h] / carry[h][1]).T.astype(o_ref.dtype)


def fox_attention(qkv, fcum, q_col, k_col, v_col, bq=512):
    b, s, _ = qkv.shape
    g = FOX_GROUP
    w = g * HEAD_DIM
    assert q_col % g == 0 and k_col % g == 0 and v_col % g == 0
    return pl.pallas_call(
        functools.partial(_fox_kernel, bq=bq),
        grid=(b, HEADS // g, s // bq),
        in_specs=[
            pl.BlockSpec((None, bq, w), lambda bi, h, i: (bi, i, q_col // g + h)),
            pl.BlockSpec((None, s, w), lambda bi, h, i: (bi, 0, k_col // g + h), pipeline_mode=pl.Buffered(1)),
            pl.BlockSpec((None, s, w), lambda bi, h, i: (bi, 0, v_col // g + h), pipeline_mode=pl.Buffered(1)),
            pl.BlockSpec((None, g, s, LANES), lambda bi, h, i: (bi, h, 0, 0), pipeline_mode=pl.Buffered(1)),
        ],
        out_specs=pl.BlockSpec((None, bq, w), lambda bi, h, i: (bi, i, h)),
        out_shape=jax.ShapeDtypeStruct((b, s, BRANCH), BF16),
        scratch_shapes=[pltpu.VMEM((g, HEAD_DIM, s), BF16), pltpu.VMEM((g, HEAD_DIM, bq), F32)],
        compiler_params=_params("parallel", "parallel", "arbitrary"),
        name="fox_attention",
    )(qkv, qkv, qkv, fcum)


def _chunk_attn_kernel(q_ref, kp_ref, kc_ref, vp_ref, vc_ref, row_ref, o_ref, bias_ref, *, bq):
    qi = pl.program_id(2)

    @pl.when((pl.program_id(1) == 0) & (qi == 0))
    def _():
        w = 2 * bq
        toeplitz = pltpu.roll(jnp.broadcast_to(row_ref[...], (bq, w)), 0, 1, stride=1, stride_axis=0)
        qc = lax.broadcasted_iota(jnp.int32, (bq, w), 0) // CHUNK
        kc = lax.broadcasted_iota(jnp.int32, (bq, w), 1) // CHUNK - bq // CHUNK
        band = jnp.where(kc <= qc, jnp.where(kc >= qc - CA_LEFT_CHUNKS, toeplitz, NEG), NEG)
        bias_ref[...] = band

    q = q_ref[...]
    s0 = lax.dot_general(q, kp_ref[...], NT_DIMS, preferred_element_type=F32) + bias_ref[:, :bq]
    s1 = lax.dot_general(q, kc_ref[...], NT_DIMS, preferred_element_type=F32) + bias_ref[:, bq:]
    s0 = jnp.where(qi > 0, s0, NEG)
    m = jnp.maximum(jnp.max(s0, axis=1, keepdims=True), jnp.max(s1, axis=1, keepdims=True))
    p0 = jnp.exp(s0 - m)
    p1 = jnp.exp(s1 - m)
    l = jnp.sum(p0, axis=1, keepdims=True) + jnp.sum(p1, axis=1, keepdims=True)
    o = (jnp.dot(p0.astype(BF16), vp_ref[...], preferred_element_type=F32)
         + jnp.dot(p1.astype(BF16), vc_ref[...], preferred_element_type=F32))
    o_ref[...] = (o / l).astype(o_ref.dtype)


def _distance_rows(rel_bias, bq):
    heads, table = rel_bias.shape
    assert table == (CHUNK - 1) + REL_CLIP + 1 and bq >= REL_CLIP
    far = rel_bias[:, table - 1:]
    return jnp.concatenate([
        jnp.broadcast_to(far, (heads, bq - REL_CLIP)),
        rel_bias[:, ::-1],
        jnp.broadcast_to(far, (heads, bq - CHUNK)),
    ], axis=1).astype(F32).reshape(heads, 1, 2 * bq)


def chunk_attention(qkv, rel_bias, q_col, k_col, v_col, bq=512):
    b, s, _ = qkv.shape
    assert bq >= CA_LEFT_CHUNKS * CHUNK
    rows = _distance_rows(rel_bias, bq)
    cur = lambda col: pl.BlockSpec((None, bq, HEAD_DIM), lambda h, bi, i: (bi, i, col + h))
    prev = lambda col: pl.BlockSpec((None, bq, HEAD_DIM), lambda h, bi, i: (bi, jnp.maximum(i - 1, 0), col + h))
    return pl.pallas_call(
        functools.partial(_chunk_attn_kernel, bq=bq),
        grid=(HEADS, b, s // bq),
        in_specs=[cur(q_col), prev(k_col), cur(k_col), prev(v_col), cur(v_col),
                  pl.BlockSpec((None, 1, 2 * bq), lambda h, bi, i: (h, 0, 0))],
        out_specs=pl.BlockSpec((None, bq, HEAD_DIM), lambda h, bi, i: (bi, i, h)),
        out_shape=jax.ShapeDtypeStruct((b, s, BRANCH), BF16),
        scratch_shapes=[pltpu.VMEM((bq, 2 * bq), F32)],
        compiler_params=_params("arbitrary", "arbitrary", "arbitrary"),
        name="chunk_attention",
    )(qkv, qkv, qkv, qkv, qkv, rows)


def _outproj_ln_kernel(a_ref, b_ref, wa_ref, wb_ref, x_ref, g_ref, beta_ref, o_ref):
    mix = (jnp.dot(a_ref[...], wa_ref[...], preferred_element_type=F32)
           + jnp.dot(b_ref[...], wb_ref[...], preferred_element_type=F32))
    o_ref[...] = _layer_norm_rows(ALPHA * x_ref[...] + mix, g_ref[...], beta_ref[...])


def outproj_ln(a, b, w_out, x, g, beta, bm=512):
    t, d = x.shape
    wa = w_out[:BRANCH].astype(BF16)
    wb = w_out[BRANCH:].astype(BF16)
    vec = pl.BlockSpec((1, d), lambda i: (0, 0))
    wspec = pl.BlockSpec((BRANCH, d), lambda i: (0, 0))
    act = pl.BlockSpec((bm, BRANCH), lambda i: (i, 0))
    return pl.pallas_call(
        _outproj_ln_kernel,
        grid=(t // bm,),
        in_specs=[act, act, wspec, wspec, pl.BlockSpec((bm, d), lambda i: (i, 0)), vec, vec],
        out_specs=pl.BlockSpec((bm, d), lambda i: (i, 0)),
        out_shape=jax.ShapeDtypeStruct((t, d), F32),
        compiler_params=_params("parallel"),
        name="outproj_ln",
    )(a, b, wa, wb, x, g.reshape(1, d), beta.reshape(1, d))


def _mlp_ln_kernel(x_ref, w1_ref, w2_ref, g_ref, beta_ref, o_ref, xb_ref, acc_ref):
    j = pl.program_id(1)

    @pl.when(j == 0)
    def _():
        xb_ref[...] = x_ref[...].astype(BF16)
        acc_ref[...] = jnp.zeros_like(acc_ref)

    h = jnp.maximum(jnp.dot(xb_ref[...], w1_ref[...], preferred_element_type=F32), 0.0)
    acc_ref[...] += jnp.dot((h * h).astype(BF16), w2_ref[...], preferred_element_type=F32)

    @pl.when(j == pl.num_programs(1) - 1)
    def _():
        o_ref[...] = _layer_norm_rows(ALPHA * x_ref[...] + acc_ref[...], g_ref[...], beta_ref[...])


def mlp_ln(x, w1, w2, g, beta, bm=512, bf=1024):
    t, d = x.shape
    ff = w1.shape[1]
    vec = pl.BlockSpec((1, d), lambda i, j: (0, 0))
    return pl.pallas_call(
        _mlp_ln_kernel,
        grid=(t // bm, ff // bf),
        in_specs=[
            pl.BlockSpec((bm, d), lambda i, j: (i, 0)),
            pl.BlockSpec((d, bf), lambda i, j: (0, j)),
            pl.BlockSpec((bf, d), lambda i, j: (j, 0)),
            vec, vec,
        ],
        out_specs=pl.BlockSpec((bm, d), lambda i, j: (i, 0)),
        out_shape=jax.ShapeDtypeStruct((t, d), F32),
        scratch_shapes=[pltpu.VMEM((bm, d), BF16), pltpu.VMEM((bm, d), F32)],
        compiler_params=_params("parallel", "arbitrary"),
        name="mlp_ln",
    )(x, w1.astype(BF16), w2.astype(BF16), g.reshape(1, d), beta.reshape(1, d))


def even_mixer(x, w_in, conv_w, conv_b, conv_ln_g, conv_ln_b, lb_logits, slot, gnorm_g):
    b, s, d = x.shape
    n = w_in.shape[1]
    u = inproj(x.reshape(b * s, d), w_in.astype(BF16), jnp.ones((1, n), F32), F32).reshape(b, s, n)
    a_out = conv_branch(u, conv_w, conv_b, conv_ln_g, conv_ln_b)
    b_out = hgrn2(u, lb_logits, slot, gnorm_g)
    return a_out.reshape(b * s, BRANCH), b_out.reshape(b * s, BRANCH)


def odd_mixer(x, w_in, b_f, rel_bias):
    b, s, d = x.shape
    w_qkv = jnp.concatenate([w_in[:, :3 * BRANCH], w_in[:, 3 * BRANCH + HEADS:]], axis=1).astype(BF16)
    w_f = w_in[:, 3 * BRANCH:3 * BRANCH + HEADS]
    scale = HEAD_DIM ** -0.5
    ones = jnp.ones((BRANCH,), F32)
    colscale = jnp.concatenate([ones * (scale * LOG2E), ones, ones, ones * scale, ones, ones]).reshape(1, 6 * BRANCH)
    qkv = inproj(x.reshape(b * s, d), w_qkv, colscale, BF16).reshape(b, s, 6 * BRANCH)
    fcum = forget_cumsum(x, w_f, b_f)
    c_out = fox_attention(qkv, fcum, 0, HEADS, 2 * HEADS)
    d_out = chunk_attention(qkv, rel_bias, 3 * HEADS, 4 * HEADS, 5 * HEADS)
    return c_out.reshape(b * s, BRANCH), d_out.reshape(b * s, BRANCH)


def kernel(x, ev_w_in, ev_conv_w, ev_conv_b, ev_conv_ln_g, ev_conv_ln_b, hgrn_lb_logits, ev_gnorm_g, ev_w_out, od_w_in, fox_b_f, rel_bias, od_w_out, ln_mix_g, ln_mix_b, mlp_w1, mlp_w2, ln_mlp_g, ln_mlp_b):
    b, s, d = x.shape
    for l in range(DEPTH):
        j = l // 2
        if l % 2 == 0:
            p, r = even_mixer(x, ev_w_in[j], ev_conv_w[j], ev_conv_b[j], ev_conv_ln_g[j], ev_conv_ln_b[j],
                              hgrn_lb_logits, j, ev_gnorm_g[j])
            w_out = ev_w_out[j]
        else:
            p, r = odd_mixer(x, od_w_in[j], fox_b_f[j], rel_bias[j])
            w_out = od_w_out[j]
        x2 = outproj_ln(p, r, w_out, x.reshape(b * s, d), ln_mix_g[l], ln_mix_b[l])
        x2 = mlp_ln(x2, mlp_w1[l], mlp_w2[l], ln_mlp_g[l], ln_mlp_b[l])
        x = x2.reshape(b, s, d)
    return x
```

```python
import functools

import jax
import jax.numpy as jnp
from jax import lax
from jax.experimental import pallas as pl
from jax.experimental.pallas import tpu as pltpu

F32 = jnp.float32
BF16 = jnp.bfloat16

D_MODEL = 2048
DEPTH = 2
CHUNK = 64
LN_EPS = 1e-5
ALPHA = (2 * DEPTH) ** 0.25
HEADS = 8
HEAD_DIM = 128
BRANCH = HEADS * HEAD_DIM
CONV_WIDTH = 31
CONV_HALO = 32
CA_LEFT_CHUNKS = 8
REL_CLIP = 256
D_FF = 4 * D_MODEL
NEG = -1e30
LOG2E = 1.4426950408889634

LANES = 128
SUBLANES = 8
VMEM_LIMIT = 56 * 1024 * 1024

NT_DIMS = (((1,), (1,)), ((), ()))
TN_DIMS = (((0,), (0,)), ((), ()))


def _params(*sem):
    return pltpu.CompilerParams(dimension_semantics=sem, vmem_limit_bytes=VMEM_LIMIT)


def _sigmoid(x):
    return 1.0 / (1.0 + jnp.exp(-x))


def _layer_norm_rows(y, g, b):
    mu = jnp.mean(y, axis=-1, keepdims=True)
    d = y - mu
    var = jnp.mean(d * d, axis=-1, keepdims=True)
    return d * lax.rsqrt(var + LN_EPS) * g + b


def _split3(x):
    hi = x.astype(BF16)
    r = x - hi.astype(F32)
    mid = r.astype(BF16)
    lo = (r - mid.astype(F32)).astype(BF16)
    return hi, mid, lo


def _cast_kernel(w_ref, o_ref):
    o_ref[...] = w_ref[...].astype(o_ref.dtype)


def cast_layer_bf16(w, layer, br=512, bc=2048):
    _, r, c = w.shape
    br, bc = min(br, r), min(bc, c)
    return pl.pallas_call(
        _cast_kernel,
        grid=(r // br, c // bc),
        in_specs=[pl.BlockSpec((None, br, bc), lambda i, j: (layer, i, j))],
        out_specs=pl.BlockSpec((br, bc), lambda i, j: (i, j)),
        out_shape=jax.ShapeDtypeStruct((r, c), BF16),
        compiler_params=_params("parallel", "parallel"),
        name="cast_bf16",
    )(w)


def _inproj_kernel(x_ref, w_ref, s_ref, o_ref, xb_ref):
    @pl.when(pl.program_id(1) == 0)
    def _():
        xb_ref[...] = x_ref[...].astype(BF16)

    acc = jnp.dot(xb_ref[...], w_ref[...], preferred_element_type=F32)
    o_ref[...] = (acc * s_ref[...]).astype(o_ref.dtype)


def inproj(x, w, colscale, out_dtype, bm=1024, bn=1024):
    t, k = x.shape
    n = w.shape[1]
    return pl.pallas_call(
        _inproj_kernel,
        grid=(t // bm, n // bn),
        in_specs=[
            pl.BlockSpec((bm, k), lambda i, j: (i, 0)),
            pl.BlockSpec((k, bn), lambda i, j: (0, j)),
            pl.BlockSpec((1, bn), lambda i, j: (0, j)),
        ],
        out_specs=pl.BlockSpec((bm, bn), lambda i, j: (i, j)),
        out_shape=jax.ShapeDtypeStruct((t, n), out_dtype),
        scratch_shapes=[pltpu.VMEM((bm, k), BF16)],
        compiler_params=_params("parallel", "arbitrary"),
        name="inproj",
    )(x, w, colscale)


CONV_ROWS = 32


def _conv_kernel(a_ref, g_ref, ah_ref, gh_ref, w_ref, cb_ref, lg_ref, lb_ref, o_ref, hext_ref, hs_ref, y_ref, *, bt):
    i = pl.program_id(1)
    halo = ah_ref[...] * _sigmoid(gh_ref[...])
    hext_ref[0:CONV_HALO, :] = jnp.where(i > 0, halo, 0.0)
    hext_ref[CONV_HALO:CONV_HALO + bt, :] = a_ref[...] * _sigmoid(g_ref[...])
    span = bt + CONV_HALO - SUBLANES
    for d in range(1, SUBLANES):
        hs_ref[d - 1, 0:span, :] = hext_ref[d:d + span, :]

    first = CONV_HALO - (CONV_WIDTH - 1)

    def rows(c, carry):
        base = pl.multiple_of(c * CONV_ROWS, CONV_ROWS)
        for cg in range(BRANCH // 256):
            cs = slice(cg * 256, (cg + 1) * 256)
            acc = jnp.zeros((CONV_ROWS, 256), F32) + cb_ref[:, cs]
            for j in range(CONV_WIDTH):
                off = first + j
                d, al = off % SUBLANES, off - off % SUBLANES
                if d == 0:
                    tap = hext_ref[pl.ds(base + al, CONV_ROWS), cs]
                else:
                    tap = hs_ref[d - 1, pl.ds(base + al, CONV_ROWS), cs]
                acc = acc + w_ref[j:j + 1, cs] * tap
            y_ref[pl.ds(base, CONV_ROWS), cs] = acc
        return carry

    lax.fori_loop(0, bt // CONV_ROWS, rows, 0)
    z = _layer_norm_rows(y_ref[...], lg_ref[...], lb_ref[...])
    o_ref[...] = (z * _sigmoid(z)).astype(o_ref.dtype)


def conv_branch(u, conv_w, conv_b, ln_g, ln_b, bt=512):
    b, s, _ = u.shape
    hb = bt // CONV_HALO
    w = jnp.pad(conv_w, ((0, CONV_HALO - CONV_WIDTH), (0, 0)))
    row = lambda v: v.reshape(1, BRANCH)
    vec = pl.BlockSpec((1, BRANCH), lambda bi, i: (0, 0))
    return pl.pallas_call(
        functools.partial(_conv_kernel, bt=bt),
        grid=(b, s // bt),
        in_specs=[
            pl.BlockSpec((None, bt, BRANCH), lambda bi, i: (bi, i, 0)),
            pl.BlockSpec((None, bt, BRANCH), lambda bi, i: (bi, i, 1)),
            pl.BlockSpec((None, CONV_HALO, BRANCH), lambda bi, i: (bi, jnp.maximum(i * hb - 1, 0), 0)),
            pl.BlockSpec((None, CONV_HALO, BRANCH), lambda bi, i: (bi, jnp.maximum(i * hb - 1, 0), 1)),
            pl.BlockSpec((CONV_HALO, BRANCH), lambda bi, i: (0, 0)),
            vec, vec, vec,
        ],
        out_specs=pl.BlockSpec((None, bt, BRANCH), lambda bi, i: (bi, i, 0)),
        out_shape=jax.ShapeDtypeStruct((b, s, BRANCH), BF16),
        scratch_shapes=[
            pltpu.VMEM((bt + CONV_HALO, BRANCH), F32),
            pltpu.VMEM((SUBLANES - 1, bt + CONV_HALO - SUBLANES, BRANCH), F32),
            pltpu.VMEM((bt, BRANCH), F32),
        ],
        compiler_params=_params("parallel", "arbitrary"),
        name="conv_branch",
    )(u, u, u, u, w, row(conv_b), row(ln_g), row(ln_b))


HG_HALF = CHUNK // 2
HG_PAIR = 2 * HEAD_DIM


def _hgrn_kernel(q_ref, f_ref, v_ref, g_ref, lbl_ref, gn_ref, o_ref, state_ref, *, slot):
    @pl.when(pl.program_id(1) == 0)
    def _():
        state_ref[...] = jnp.zeros_like(state_ref)

    c, hh = CHUNK, HG_HALF
    lbl = lbl_ref[...]
    e = jnp.exp(lbl - jnp.max(lbl, axis=0, keepdims=True))
    upto = lax.broadcasted_iota(jnp.int32, e.shape, 0) <= slot
    lb = jnp.sum(jnp.where(upto, e, 0.0), axis=0, keepdims=True) / jnp.sum(e, axis=0, keepdims=True)
    f = lb + (1.0 - lb) * _sigmoid(f_ref[...])
    lf = jnp.log(f)
    kk = 1.0 - f
    q = q_ref[...]
    qs = q * _sigmoid(q)
    r_i = lax.broadcasted_iota(jnp.int32, (c, c), 0)
    c_i = lax.broadcasted_iota(jnp.int32, (c, c), 1)
    tri = jnp.where(c_i <= r_i, 1.0, 0.0).astype(BF16)
    l3 = jnp.dot(tri, jnp.concatenate(_split3(lf), axis=1), preferred_element_type=F32)
    ll = l3[:, :BRANCH] + l3[:, BRANCH:2 * BRANCH] + l3[:, 2 * BRANCH:]
    l_mid0 = ll[hh // 2 - 1:hh // 2]
    l_edge = ll[hh - 1:hh]
    l_mid1 = ll[hh + hh // 2 - 1:hh + hh // 2]
    l_end = ll[c - 1:c]
    lt, lbt = ll[:hh], ll[hh:]
    a_d0 = qs[:hh] * jnp.exp(lt - l_mid0)
    b_d0 = kk[:hh] * jnp.exp(l_mid0 - lt)
    a_off = qs[hh:] * jnp.exp(lbt - l_edge)
    b_off = kk[:hh] * jnp.exp(l_edge - lt)
    a_d1 = qs[hh:] * jnp.exp(lbt - l_mid1)
    b_d1 = kk[hh:] * jnp.exp(l_mid1 - lbt)
    a_int = (qs * jnp.exp(ll)).astype(BF16)
    kd = (kk * jnp.exp(l_end - ll)).astype(BF16)
    dec = jnp.exp(l_end)

    zero = jnp.zeros((hh, HG_PAIR), F32)
    first = lax.broadcasted_iota(jnp.int32, (c, HG_PAIR), 1) < HEAD_DIM
    first3 = jnp.concatenate([first, first, first], axis=1)
    row2 = lax.broadcasted_iota(jnp.int32, (HG_PAIR, HG_PAIR), 0) < HEAD_DIM
    col2 = lax.broadcasted_iota(jnp.int32, (HG_PAIR, HG_PAIR), 1) < HEAD_DIM
    same_head = row2 == col2
    key_i = lax.broadcasted_iota(jnp.int32, (c, 2 * c), 1) % c
    qry_i = lax.broadcasted_iota(jnp.int32, (c, 2 * c), 0)
    stack = lambda top, bottom: jnp.concatenate([top, bottom], axis=0)

    pairs = range(HEADS // 2)
    pss = [slice(p * HG_PAIR, (p + 1) * HG_PAIR) for p in pairs]
    scs, sts, outs = [], [], []
    for p in pairs:
        ps = pss[p]
        lhs = jnp.concatenate([stack(a_d0[:, ps], zero), stack(zero, a_off[:, ps]), stack(zero, a_d1[:, ps])], axis=1)
        keys = jnp.concatenate([stack(b_d0[:, ps], zero), stack(b_off[:, ps], zero), stack(zero, b_d1[:, ps])], axis=1)
        rhs_t = stack(jnp.where(first3, keys, 0.0), jnp.where(first3, 0.0, keys))
        sc = lax.dot_general(lhs.astype(BF16), rhs_t.astype(BF16), NT_DIMS, preferred_element_type=F32)
        scs.append(jnp.where(key_i <= qry_i, sc, 0.0))
    for p in pairs:
        st = state_ref[p]
        sts.append(lax.dot_general(a_int[:, pss[p]], st.astype(BF16), NT_DIMS, preferred_element_type=F32))
        upd = lax.dot_general(v_ref[:, pss[p]].astype(BF16), kd[:, pss[p]], TN_DIMS, preferred_element_type=F32)
        state_ref[p] = dec[:, pss[p]] * st + jnp.where(same_head, upd, 0.0)
    for p in pairs:
        v = v_ref[:, pss[p]]
        v_bd = stack(jnp.where(first, v, 0.0), jnp.where(first, 0.0, v)).astype(BF16)
        outs.append(jnp.dot(scs[p].astype(BF16), v_bd, preferred_element_type=F32) + sts[p])
    for p in pairs:
        o = outs[p]
        for j in range(2):
            hsl = slice(p * HG_PAIR + j * HEAD_DIM, p * HG_PAIR + (j + 1) * HEAD_DIM)
            oj = o[:, j * HEAD_DIM:(j + 1) * HEAD_DIM]
            oj = oj * lax.rsqrt(jnp.mean(oj * oj, axis=-1, keepdims=True) + LN_EPS)
            g = g_ref[:, hsl]
            o_ref[:, hsl] = (oj * gn_ref[:, hsl] * (g * _sigmoid(g))).astype(o_ref.dtype)


def hgrn2(u, lb_logits, slot, gnorm_g):
    b, s, _ = u.shape
    slots = lb_logits.shape[0]
    col = lambda j: pl.BlockSpec((None, CHUNK, BRANCH), lambda bi, i: (bi, i, j))
    vec = pl.BlockSpec((1, BRANCH), lambda bi, i: (0, 0))
    return pl.pallas_call(
        functools.partial(_hgrn_kernel, slot=slot),
        grid=(b, s // CHUNK),
        in_specs=[col(2), col(3), col(4), col(5), pl.BlockSpec((slots, BRANCH), lambda bi, i: (0, 0)), vec],
        out_specs=pl.BlockSpec((None, CHUNK, BRANCH), lambda bi, i: (bi, i, 0)),
        out_shape=jax.ShapeDtypeStruct((b, s, BRANCH), BF16),
        scratch_shapes=[pltpu.VMEM((HEADS // 2, HG_PAIR, HG_PAIR), F32)],
        compiler_params=_params("parallel", "arbitrary"),
        name="hgrn2",
    )(u, u, u, u, lb_logits.astype(F32), gnorm_g.reshape(1, BRANCH))


def _fgate_kernel(x_ref, w_ref, b_ref, o_ref, carry_ref, *, bm):
    @pl.when(pl.program_id(1) == 0)
    def _():
        carry_ref[...] = jnp.zeros_like(carry_ref)

    z = jnp.dot(x_ref[...].astype(BF16), w_ref[...], preferred_element_type=F32) + b_ref[...]
    ls = jnp.minimum(z, 0.0) - jnp.log(1.0 + jnp.exp(-jnp.abs(z)))
    r_i = lax.broadcasted_iota(jnp.int32, (bm, bm), 0)
    c_i = lax.broadcasted_iota(jnp.int32, (bm, bm), 1)
    tri = jnp.where(c_i <= r_i, 1.0, 0.0).astype(BF16)
    hi, mid, lo = _split3(ls)
    cum = (jnp.dot(tri, hi, preferred_element_type=F32)
           + jnp.dot(tri, mid, preferred_element_type=F32)
           + jnp.dot(tri, lo, preferred_element_type=F32)) + carry_ref[...]
    carry_ref[...] = cum[bm - 1:bm, :]
    cum2 = cum * LOG2E
    for h in range(HEADS):
        o_ref[h] = jnp.broadcast_to(cum2[:, h:h + 1], (bm, LANES))


def forget_cumsum(x, w_f, b_f, bm=512):
    b, s, d = x.shape
    w = jnp.pad(w_f, ((0, 0), (0, LANES - HEADS))).astype(BF16)
    bias = jnp.pad(b_f, (0, LANES - HEADS)).reshape(1, LANES).astype(F32)
    return pl.pallas_call(
        functools.partial(_fgate_kernel, bm=bm),
        grid=(b, s // bm),
        in_specs=[
            pl.BlockSpec((None, bm, d), lambda bi, i: (bi, i, 0)),
            pl.BlockSpec((d, LANES), lambda bi, i: (0, 0)),
            pl.BlockSpec((1, LANES), lambda bi, i: (0, 0)),
        ],
        out_specs=pl.BlockSpec((None, HEADS, bm, LANES), lambda bi, i: (bi, 0, i, 0)),
        out_shape=jax.ShapeDtypeStruct((b, HEADS, s, LANES), F32),
        scratch_shapes=[pltpu.VMEM((1, LANES), F32)],
        compiler_params=_params("parallel", "arbitrary"),
        name="forget_cumsum",
    )(x, w, bias)


FOX_GROUP = 4


def _fox_kernel(q_ref, k_ref, v_ref, f_ref, o_ref, vt_ref, acc_ref, *, bq):
    qi = pl.program_id(2)
    nblk = pl.num_programs(2)
    d = HEAD_DIM
    heads = range(FOX_GROUP)
    hs = [slice(h * d, (h + 1) * d) for h in heads]

    @pl.when(qi == 0)
    def _():
        def tr(c, carry):
            cs = pl.multiple_of(c * bq, bq)
            for h in heads:
                vt_ref[h, :, pl.ds(cs, bq)] = v_ref[pl.ds(cs, bq), hs[h]].astype(F32).T.astype(BF16)
            return carry
        lax.fori_loop(0, nblk, tr, 0)

    acc_ref[...] = jnp.zeros_like(acc_ref)

    def scores(h, kb):
        ks = pl.multiple_of(kb * bq, bq)
        st = lax.dot_general(k_ref[pl.ds(ks, bq), hs[h]], q_ref[:, hs[h]], NT_DIMS, preferred_element_type=F32)
        return st - jnp.tile(f_ref[h, pl.ds(ks, bq), :], (1, bq // LANES))

    def values(h, kb, p):
        ks = pl.multiple_of(kb * bq, bq)
        return jnp.dot(vt_ref[h, :, pl.ds(ks, bq)], p, preferred_element_type=F32)

    def softmax_step(st, m, l):
        m_new = jnp.maximum(m, jnp.max(st, axis=0, keepdims=True))
        alpha = jnp.exp2(m - m_new)
        p = jnp.exp2(st - m_new)
        return m_new, alpha, alpha * l + jnp.sum(p, axis=0, keepdims=True), p.astype(BF16)

    def step(kb, carry, diagonal):
        sts = [scores(h, kb) for h in heads]
        if diagonal:
            keep = (lax.broadcasted_iota(jnp.int32, (bq, bq), 0) <= lax.broadcasted_iota(jnp.int32, (bq, bq), 1))
            sts = [jnp.where(keep, st, NEG) for st in sts]
        out = []
        for h in heads:
            m, l = carry[h]
            m, alpha, l, p = softmax_step(sts[h], m, l)
            acc_ref[h] = alpha * acc_ref[h] + values(h, kb, p)
            out.append((m, l))
        return tuple(out)

    init = tuple((jnp.full((1, bq), NEG, F32), jnp.zeros((1, bq), F32)) for _ in heads)
    carry = lax.fori_loop(0, qi, lambda kb, cr: step(kb, cr, False), init)
    carry = step(qi, carry, True)
    for h in heads:
        o_ref[:, hs[h]] = (acc_ref[h] / carry[h][1]).T.astype(o_ref.dtype)


def fox_attention(qkv, fcum, q_col, k_col, v_col, bq=512):
    b, s, _ = qkv.shape
    g = FOX_GROUP
    w = g * HEAD_DIM
    assert q_col % g == 0 and k_col % g == 0 and v_col % g == 0
    return pl.pallas_call(
        functools.partial(_fox_kernel, bq=bq),
        grid=(b, HEADS // g, s // bq),
        in_specs=[
            pl.BlockSpec((None, bq, w), lambda bi, h, i: (bi, i, q_col // g + h)),
            pl.BlockSpec((None, s, w), lambda bi, h, i: (bi, 0, k_col // g + h), pipeline_mode=pl.Buffered(1)),
            pl.BlockSpec((None, s, w), lambda bi, h, i: (bi, 0, v_col // g + h), pipeline_mode=pl.Buffered(1)),
            pl.BlockSpec((None, g, s, LANES), lambda bi, h, i: (bi, h, 0, 0), pipeline_mode=pl.Buffered(1)),
        ],
        out_specs=pl.BlockSpec((None, bq, w), lambda bi, h, i: (bi, i, h)),
        out_shape=jax.ShapeDtypeStruct((b, s, BRANCH), BF16),
        scratch_shapes=[pltpu.VMEM((g, HEAD_DIM, s), BF16), pltpu.VMEM((g, HEAD_DIM, bq), F32)],
        compiler_params=_params("parallel", "parallel", "arbitrary"),
        name="fox_attention",
    )(qkv, qkv, qkv, fcum)


CA_GROUP = 4


def _chunk_attn_kernel(q_ref, kp_ref, kc_ref, vp_ref, vc_ref, row_ref, o_ref, bias_ref, *, bq):
    qi = pl.program_id(2)
    heads = range(CA_GROUP)
    hs = [slice(h * HEAD_DIM, (h + 1) * HEAD_DIM) for h in heads]

    @pl.when((pl.program_id(1) == 0) & (qi == 0))
    def _():
        w = 2 * bq
        qc = lax.broadcasted_iota(jnp.int32, (bq, w), 0) // CHUNK
        kc = lax.broadcasted_iota(jnp.int32, (bq, w), 1) // CHUNK - bq // CHUNK
        for h in heads:
            toeplitz = pltpu.roll(jnp.broadcast_to(row_ref[h], (bq, w)), 0, 1, stride=1, stride_axis=0)
            band = jnp.where(kc <= qc, jnp.where(kc >= qc - CA_LEFT_CHUNKS, toeplitz, NEG), NEG)
            bias_ref[h] = band.T

    sts = []
    for h in heads:
        q = q_ref[:, hs[h]]
        sp = lax.dot_general(kp_ref[:, hs[h]], q, NT_DIMS, preferred_element_type=F32) + bias_ref[h, :bq, :]
        sc = lax.dot_general(kc_ref[:, hs[h]], q, NT_DIMS, preferred_element_type=F32) + bias_ref[h, bq:, :]
        sts.append((jnp.where(qi > 0, sp, NEG), sc))
    for h in heads:
        sp, sc = sts[h]
        m = jnp.maximum(jnp.max(sp, axis=0, keepdims=True), jnp.max(sc, axis=0, keepdims=True))
        pp = jnp.exp(sp - m)
        pc = jnp.exp(sc - m)
        l = jnp.sum(pp, axis=0, keepdims=True) + jnp.sum(pc, axis=0, keepdims=True)
        vtp = vp_ref[:, hs[h]].astype(F32).T.astype(BF16)
        vtc = vc_ref[:, hs[h]].astype(F32).T.astype(BF16)
        o = (jnp.dot(vtp, pp.astype(BF16), preferred_element_type=F32)
             + jnp.dot(vtc, pc.astype(BF16), preferred_element_type=F32))
        o_ref[:, hs[h]] = (o / l).T.astype(o_ref.dtype)


def _distance_rows(rel_bias, bq):
    heads, table = rel_bias.shape
    assert table == (CHUNK - 1) + REL_CLIP + 1 and bq >= REL_CLIP
    far = rel_bias[:, table - 1:]
    return jnp.concatenate([
        jnp.broadcast_to(far, (heads, bq - REL_CLIP)),
        rel_bias[:, ::-1],
        jnp.broadcast_to(far, (heads, bq - CHUNK)),
    ], axis=1).astype(F32).reshape(heads, 1, 2 * bq)


def chunk_attention(qkv, rel_bias, q_col, k_col, v_col, bq=512):
    b, s, _ = qkv.shape
    assert bq >= CA_LEFT_CHUNKS * CHUNK
    rows = _distance_rows(rel_bias, bq)
    g = CA_GROUP
    w = g * HEAD_DIM
    assert q_col % g == 0 and k_col % g == 0 and v_col % g == 0
    cur = lambda col: pl.BlockSpec((None, bq, w), lambda h, bi, i: (bi, i, col // g + h))
    prev = lambda col: pl.BlockSpec((None, bq, w), lambda h, bi, i: (bi, jnp.maximum(i - 1, 0), col // g + h))
    return pl.pallas_call(
        functools.partial(_chunk_attn_kernel, bq=bq),
        grid=(HEADS // g, b, s // bq),
        in_specs=[cur(q_col), prev(k_col), cur(k_col), prev(v_col), cur(v_col),
                  pl.BlockSpec((g, 1, 2 * bq), lambda h, bi, i: (h, 0, 0))],
        out_specs=pl.BlockSpec((None, bq, w), lambda h, bi, i: (bi, i, h)),
        out_shape=jax.ShapeDtypeStruct((b, s, BRANCH), BF16),
        scratch_shapes=[pltpu.VMEM((g, 2 * bq, bq), F32)],
        compiler_params=_params("arbitrary", "arbitrary", "arbitrary"),
        name="chunk_attention",
    )(qkv, qkv, qkv, qkv, qkv, rows)


def _outproj_ln_kernel(a_ref, b_ref, wa_ref, wb_ref, x_ref, g_ref, beta_ref, o_ref):
    mix = (jnp.dot(a_ref[...], wa_ref[...], preferred_element_type=F32)
           + jnp.dot(b_ref[...], wb_ref[...], preferred_element_type=F32))
    o_ref[...] = _layer_norm_rows(ALPHA * x_ref[...] + mix, g_ref[...], beta_ref[...])


def outproj_ln(a, b, w_out, x, g, beta, bm=512):
    t, d = x.shape
    vec = pl.BlockSpec((1, d), lambda i: (0, 0))
    act = pl.BlockSpec((bm, BRANCH), lambda i: (i, 0))
    return pl.pallas_call(
        _outproj_ln_kernel,
        grid=(t // bm,),
        in_specs=[act, act,
                  pl.BlockSpec((BRANCH, d), lambda i: (0, 0)), pl.BlockSpec((BRANCH, d), lambda i: (1, 0)),
                  pl.BlockSpec((bm, d), lambda i: (i, 0)), vec, vec],
        out_specs=pl.BlockSpec((bm, d), lambda i: (i, 0)),
        out_shape=jax.ShapeDtypeStruct((t, d), F32),
        compiler_params=_params("parallel"),
        name="outproj_ln",
    )(a, b, w_out, w_out, x, g.reshape(1, d), beta.reshape(1, d))


def _mlp_ln_kernel(x_ref, w1_ref, w2_ref, g_ref, beta_ref, o_ref, xb_ref, acc_ref):
    j = pl.program_id(1)

    @pl.when(j == 0)
    def _():
        xb_ref[...] = x_ref[...].astype(BF16)
        acc_ref[...] = jnp.zeros_like(acc_ref)

    h = jnp.maximum(jnp.dot(xb_ref[...], w1_ref[...], preferred_element_type=F32), 0.0)
    acc_ref[...] += jnp.dot((h * h).astype(BF16), w2_ref[...], preferred_element_type=F32)

    @pl.when(j == pl.num_programs(1) - 1)
    def _():
        o_ref[...] = _layer_norm_rows(ALPHA * x_ref[...] + acc_ref[...], g_ref[...], beta_ref[...])


def mlp_ln(x, w1, w2, g, beta, bm=512, bf=1024):
    t, d = x.shape
    ff = w1.shape[1]
    vec = pl.BlockSpec((1, d), lambda i, j: (0, 0))
    return pl.pallas_call(
        _mlp_ln_kernel,
        grid=(t // bm, ff // bf),
        in_specs=[
            pl.BlockSpec((bm, d), lambda i, j: (i, 0)),
            pl.BlockSpec((d, bf), lambda i, j: (0, j)),
            pl.BlockSpec((bf, d), lambda i, j: (j, 0)),
            vec, vec,
        ],
        out_specs=pl.BlockSpec((bm, d), lambda i, j: (i, 0)),
        out_shape=jax.ShapeDtypeStruct((t, d), F32),
        scratch_shapes=[pltpu.VMEM((bm, d), BF16), pltpu.VMEM((bm, d), F32)],
        compiler_params=_params("parallel", "arbitrary"),
        name="mlp_ln",
    )(x, w1, w2, g.reshape(1, d), beta.reshape(1, d))


def even_mixer(x, w_in, conv_w, conv_b, conv_ln_g, conv_ln_b, lb_logits, slot, gnorm_g):
    b, s, d = x.shape
    n = w_in.shape[1]
    u = inproj(x.reshape(b * s, d), w_in, jnp.ones((1, n), F32), F32).reshape(b, s, n)
    a_out = conv_branch(u, conv_w, conv_b, conv_ln_g, conv_ln_b)
    b_out = hgrn2(u, lb_logits, slot, gnorm_g)
    return a_out.reshape(b * s, BRANCH), b_out.reshape(b * s, BRANCH)


def odd_mixer(x, w_in, b_f, rel_bias):
    b, s, d = x.shape
    w_qkv = jnp.concatenate([w_in[:, :3 * BRANCH], w_in[:, 3 * BRANCH + HEADS:]], axis=1).astype(BF16)
    w_f = w_in[:, 3 * BRANCH:3 * BRANCH + HEADS]
    scale = HEAD_DIM ** -0.5
    ones = jnp.ones((BRANCH,), F32)
    colscale = jnp.concatenate([ones * (scale * LOG2E), ones, ones, ones * scale, ones, ones]).reshape(1, 6 * BRANCH)
    qkv = inproj(x.reshape(b * s, d), w_qkv, colscale, BF16).reshape(b, s, 6 * BRANCH)
    fcum = forget_cumsum(x, w_f, b_f)
    c_out = fox_attention(qkv, fcum, 0, HEADS, 2 * HEADS)
    d_out = chunk_attention(qkv, rel_bias, 3 * HEADS, 4 * HEADS, 5 * HEADS)
    return c_out.reshape(b * s, BRANCH), d_out.reshape(b * s, BRANCH)


def kernel(x, ev_w_in, ev_conv_w, ev_conv_b, ev_conv_ln_g, ev_conv_ln_b, hgrn_lb_logits, ev_gnorm_g, ev_w_out, od_w_in, fox_b_f, rel_bias, od_w_out, ln_mix_g, ln_mix_b, mlp_w1, mlp_w2, ln_mlp_g, ln_mlp_b):
    b, s, d = x.shape
    for l in range(DEPTH):
        j = l // 2
        if l % 2 == 0:
            p, r = even_mixer(x, cast_layer_bf16(ev_w_in, j), ev_conv_w[j], ev_conv_b[j], ev_conv_ln_g[j],
                              ev_conv_ln_b[j], hgrn_lb_logits, j, ev_gnorm_g[j])
            w_out = cast_layer_bf16(ev_w_out, j)
        else:
            p, r = odd_mixer(x, od_w_in[j], fox_b_f[j], rel_bias[j])
            w_out = cast_layer_bf16(od_w_out, j)
        x2 = outproj_ln(p, r, w_out, x.reshape(b * s, d), ln_mix_g[l], ln_mix_b[l])
        x2 = mlp_ln(x2, cast_layer_bf16(mlp_w1, l), cast_layer_bf16(mlp_w2, l), ln_mlp_g[l], ln_mlp_b[l])
        x = x2.reshape(b, s, d)
    return x
```

```python
import functools

import jax
import jax.numpy as jnp
from jax import lax
from jax.experimental import pallas as pl
from jax.experimental.pallas import tpu as pltpu

F32 = jnp.float32
BF16 = jnp.bfloat16

D_MODEL = 2048
DEPTH = 2
CHUNK = 64
LN_EPS = 1e-5
ALPHA = (2 * DEPTH) ** 0.25
HEADS = 8
HEAD_DIM = 128
BRANCH = HEADS * HEAD_DIM
CONV_WIDTH = 31
CONV_HALO = 32
CA_LEFT_CHUNKS = 8
REL_CLIP = 256
D_FF = 4 * D_MODEL
NEG = -1e30
LOG2E = 1.4426950408889634

ONES_PAD = 16
LANES = 128
SUBLANES = 8
VMEM_LIMIT = 56 * 1024 * 1024

NT_DIMS = (((1,), (1,)), ((), ()))
TN_DIMS = (((0,), (0,)), ((), ()))


def _params(*sem):
    return pltpu.CompilerParams(dimension_semantics=sem, vmem_limit_bytes=VMEM_LIMIT)


def _sigmoid(x):
    return 1.0 / (1.0 + jnp.exp(-x))


def _layer_norm_rows(y, g, b):
    mu = jnp.mean(y, axis=-1, keepdims=True)
    d = y - mu
    var = jnp.mean(d * d, axis=-1, keepdims=True)
    return d * lax.rsqrt(var + LN_EPS) * g + b


def _split3(x):
    hi = x.astype(BF16)
    r = x - hi.astype(F32)
    mid = r.astype(BF16)
    lo = (r - mid.astype(F32)).astype(BF16)
    return hi, mid, lo


def _cast_kernel(w_ref, o_ref):
    o_ref[...] = w_ref[...].astype(o_ref.dtype)


def cast_layer_bf16(w, layer, cols=None, br=512, bc=1024):
    _, r, c = w.shape
    c = c if cols is None else cols
    br, bc = min(br, r), min(bc, c)
    return pl.pallas_call(
        _cast_kernel,
        grid=(r // br, c // bc),
        in_specs=[pl.BlockSpec((None, br, bc), lambda i, j: (layer, i, j))],
        out_specs=pl.BlockSpec((br, bc), lambda i, j: (i, j)),
        out_shape=jax.ShapeDtypeStruct((r, c), BF16),
        compiler_params=_params("parallel", "parallel"),
        name="cast_bf16",
    )(w)


def _inproj_kernel(x_ref, w_ref, s_ref, o_ref, xb_ref):
    @pl.when(pl.program_id(1) == 0)
    def _():
        xb_ref[...] = x_ref[...].astype(BF16)

    acc = jnp.dot(xb_ref[...], w_ref[...], preferred_element_type=F32)
    o_ref[...] = (acc * s_ref[...]).astype(o_ref.dtype)


def inproj(x, w, colscale, out_dtype, bm=1024, bn=1024):
    t, k = x.shape
    n = w.shape[1]
    return pl.pallas_call(
        _inproj_kernel,
        grid=(t // bm, n // bn),
        in_specs=[
            pl.BlockSpec((bm, k), lambda i, j: (i, 0)),
            pl.BlockSpec((k, bn), lambda i, j: (0, j)),
            pl.BlockSpec((1, bn), lambda i, j: (0, j)),
        ],
        out_specs=pl.BlockSpec((bm, bn), lambda i, j: (i, j)),
        out_shape=jax.ShapeDtypeStruct((t, n), out_dtype),
        scratch_shapes=[pltpu.VMEM((bm, k), BF16)],
        compiler_params=_params("parallel", "arbitrary"),
        name="inproj",
    )(x, w, colscale)


CONV_ROWS = 32


def _conv_kernel(a_ref, g_ref, ah_ref, gh_ref, w_ref, cb_ref, lg_ref, lb_ref, o_ref, hext_ref, hs_ref, y_ref, *, bt):
    i = pl.program_id(1)
    halo = ah_ref[...] * _sigmoid(gh_ref[...])
    hext_ref[0:CONV_HALO, :] = jnp.where(i > 0, halo, 0.0)
    hext_ref[CONV_HALO:CONV_HALO + bt, :] = a_ref[...] * _sigmoid(g_ref[...])
    span = bt + CONV_HALO - SUBLANES
    for d in range(1, SUBLANES):
        hs_ref[d - 1, 0:span, :] = hext_ref[d:d + span, :]

    first = CONV_HALO - (CONV_WIDTH - 1)

    def rows(c, carry):
        base = pl.multiple_of(c * CONV_ROWS, CONV_ROWS)
        for cg in range(BRANCH // 256):
            cs = slice(cg * 256, (cg + 1) * 256)
            acc = jnp.zeros((CONV_ROWS, 256), F32) + cb_ref[:, cs]
            for j in range(CONV_WIDTH):
                off = first + j
                d, al = off % SUBLANES, off - off % SUBLANES
                if d == 0:
                    tap = hext_ref[pl.ds(base + al, CONV_ROWS), cs]
                else:
                    tap = hs_ref[d - 1, pl.ds(base + al, CONV_ROWS), cs]
                acc = acc + w_ref[j:j + 1, cs] * tap
            y_ref[pl.ds(base, CONV_ROWS), cs] = acc
        return carry

    lax.fori_loop(0, bt // CONV_ROWS, rows, 0)
    z = _layer_norm_rows(y_ref[...], lg_ref[...], lb_ref[...])
    o_ref[...] = (z * _sigmoid(z)).astype(o_ref.dtype)


def conv_branch(u, conv_w, conv_b, ln_g, ln_b, bt=512):
    b, s, _ = u.shape
    hb = bt // CONV_HALO
    w = jnp.pad(conv_w, ((0, CONV_HALO - CONV_WIDTH), (0, 0)))
    row = lambda v: v.reshape(1, BRANCH)
    vec = pl.BlockSpec((1, BRANCH), lambda bi, i: (0, 0))
    return pl.pallas_call(
        functools.partial(_conv_kernel, bt=bt),
        grid=(b, s // bt),
        in_specs=[
            pl.BlockSpec((None, bt, BRANCH), lambda bi, i: (bi, i, 0)),
            pl.BlockSpec((None, bt, BRANCH), lambda bi, i: (bi, i, 1)),
            pl.BlockSpec((None, CONV_HALO, BRANCH), lambda bi, i: (bi, jnp.maximum(i * hb - 1, 0), 0)),
            pl.BlockSpec((None, CONV_HALO, BRANCH), lambda bi, i: (bi, jnp.maximum(i * hb - 1, 0), 1)),
            pl.BlockSpec((CONV_HALO, BRANCH), lambda bi, i: (0, 0)),
            vec, vec, vec,
        ],
        out_specs=pl.BlockSpec((None, bt, BRANCH), lambda bi, i: (bi, i, 0)),
        out_shape=jax.ShapeDtypeStruct((b, s, BRANCH), BF16),
        scratch_shapes=[
            pltpu.VMEM((bt + CONV_HALO, BRANCH), F32),
            pltpu.VMEM((SUBLANES - 1, bt + CONV_HALO - SUBLANES, BRANCH), F32),
            pltpu.VMEM((bt, BRANCH), F32),
        ],
        compiler_params=_params("parallel", "arbitrary"),
        name="conv_branch",
    )(u, u, u, u, w, row(conv_b), row(ln_g), row(ln_b))


HG_HALF = CHUNK // 2
HG_PAIR = 2 * HEAD_DIM


def _hgrn_kernel(q_ref, f_ref, v_ref, g_ref, lbl_ref, gn_ref, o_ref, state_ref, *, slot):
    @pl.when(pl.program_id(1) == 0)
    def _():
        state_ref[...] = jnp.zeros_like(state_ref)

    c, hh = CHUNK, HG_HALF
    lbl = lbl_ref[...]
    e = jnp.exp(lbl - jnp.max(lbl, axis=0, keepdims=True))
    upto = lax.broadcasted_iota(jnp.int32, e.shape, 0) <= slot
    lb = jnp.sum(jnp.where(upto, e, 0.0), axis=0, keepdims=True) / jnp.sum(e, axis=0, keepdims=True)
    f = lb + (1.0 - lb) * _sigmoid(f_ref[...])
    lf = jnp.log(f)
    kk = 1.0 - f
    q = q_ref[...]
    qs = q * _sigmoid(q)
    r_i = lax.broadcasted_iota(jnp.int32, (c, c), 0)
    c_i = lax.broadcasted_iota(jnp.int32, (c, c), 1)
    tri = jnp.where(c_i <= r_i, 1.0, 0.0).astype(BF16)
    l3 = jnp.dot(tri, jnp.concatenate(_split3(lf), axis=1), preferred_element_type=F32)
    ll = l3[:, :BRANCH] + l3[:, BRANCH:2 * BRANCH] + l3[:, 2 * BRANCH:]
    l_mid0 = ll[hh // 2 - 1:hh // 2]
    l_edge = ll[hh - 1:hh]
    l_mid1 = ll[hh + hh // 2 - 1:hh + hh // 2]
    l_end = ll[c - 1:c]
    lt, lbt = ll[:hh], ll[hh:]
    a_d0 = qs[:hh] * jnp.exp(lt - l_mid0)
    b_d0 = kk[:hh] * jnp.exp(l_mid0 - lt)
    a_off = qs[hh:] * jnp.exp(lbt - l_edge)
    b_off = kk[:hh] * jnp.exp(l_edge - lt)
    a_d1 = qs[hh:] * jnp.exp(lbt - l_mid1)
    b_d1 = kk[hh:] * jnp.exp(l_mid1 - lbt)
    a_int = (qs * jnp.exp(ll)).astype(BF16)
    kd = (kk * jnp.exp(l_end - ll)).astype(BF16)
    dec = jnp.exp(l_end)

    zero = jnp.zeros((hh, HG_PAIR), F32)
    first = lax.broadcasted_iota(jnp.int32, (c, HG_PAIR), 1) < HEAD_DIM
    first3 = jnp.concatenate([first, first, first], axis=1)
    row2 = lax.broadcasted_iota(jnp.int32, (HG_PAIR, HG_PAIR), 0) < HEAD_DIM
    col2 = lax.broadcasted_iota(jnp.int32, (HG_PAIR, HG_PAIR), 1) < HEAD_DIM
    same_head = row2 == col2
    key_i = lax.broadcasted_iota(jnp.int32, (c, 2 * c), 1) % c
    qry_i = lax.broadcasted_iota(jnp.int32, (c, 2 * c), 0)
    stack = lambda top, bottom: jnp.concatenate([top, bottom], axis=0)

    pairs = range(HEADS // 2)
    pss = [slice(p * HG_PAIR, (p + 1) * HG_PAIR) for p in pairs]
    scs, sts, outs = [], [], []
    for p in pairs:
        ps = pss[p]
        lhs = jnp.concatenate([stack(a_d0[:, ps], zero), stack(zero, a_off[:, ps]), stack(zero, a_d1[:, ps])], axis=1)
        keys = jnp.concatenate([stack(b_d0[:, ps], zero), stack(b_off[:, ps], zero), stack(zero, b_d1[:, ps])], axis=1)
        rhs_t = stack(jnp.where(first3, keys, 0.0), jnp.where(first3, 0.0, keys))
        sc = lax.dot_general(lhs.astype(BF16), rhs_t.astype(BF16), NT_DIMS, preferred_element_type=F32)
        scs.append(jnp.where(key_i <= qry_i, sc, 0.0))
    for p in pairs:
        st = state_ref[p]
        sts.append(lax.dot_general(a_int[:, pss[p]], st.astype(BF16), NT_DIMS, preferred_element_type=F32))
        upd = lax.dot_general(v_ref[:, pss[p]].astype(BF16), kd[:, pss[p]], TN_DIMS, preferred_element_type=F32)
        state_ref[p] = dec[:, pss[p]] * st + jnp.where(same_head, upd, 0.0)
    for p in pairs:
        v = v_ref[:, pss[p]]
        v_bd = stack(jnp.where(first, v, 0.0), jnp.where(first, 0.0, v)).astype(BF16)
        outs.append(jnp.dot(scs[p].astype(BF16), v_bd, preferred_element_type=F32) + sts[p])
    for p in pairs:
        o = outs[p]
        for j in range(2):
            hsl = slice(p * HG_PAIR + j * HEAD_DIM, p * HG_PAIR + (j + 1) * HEAD_DIM)
            oj = o[:, j * HEAD_DIM:(j + 1) * HEAD_DIM]
            oj = oj * lax.rsqrt(jnp.mean(oj * oj, axis=-1, keepdims=True) + LN_EPS)
            g = g_ref[:, hsl]
            o_ref[:, hsl] = (oj * gn_ref[:, hsl] * (g * _sigmoid(g))).astype(o_ref.dtype)


def hgrn2(u, lb_logits, slot, gnorm_g):
    b, s, _ = u.shape
    slots = lb_logits.shape[0]
    col = lambda j: pl.BlockSpec((None, CHUNK, BRANCH), lambda bi, i: (bi, i, j))
    vec = pl.BlockSpec((1, BRANCH), lambda bi, i: (0, 0))
    return pl.pallas_call(
        functools.partial(_hgrn_kernel, slot=slot),
        grid=(b, s // CHUNK),
        in_specs=[col(2), col(3), col(4), col(5), pl.BlockSpec((slots, BRANCH), lambda bi, i: (0, 0)), vec],
        out_specs=pl.BlockSpec((None, CHUNK, BRANCH), lambda bi, i: (bi, i, 0)),
        out_shape=jax.ShapeDtypeStruct((b, s, BRANCH), BF16),
        scratch_shapes=[pltpu.VMEM((HEADS // 2, HG_PAIR, HG_PAIR), F32)],
        compiler_params=_params("parallel", "arbitrary"),
        name="hgrn2",
    )(u, u, u, u, lb_logits.astype(F32), gnorm_g.reshape(1, BRANCH))


def _fgate_kernel(x_ref, w_ref, b_ref, o_ref, carry_ref, *, bm):
    @pl.when(pl.program_id(1) == 0)
    def _():
        carry_ref[...] = jnp.zeros_like(carry_ref)

    z = jnp.dot(x_ref[...].astype(BF16), w_ref[...], preferred_element_type=F32) + b_ref[...]
    ls = jnp.minimum(z, 0.0) - jnp.log(1.0 + jnp.exp(-jnp.abs(z)))
    r_i = lax.broadcasted_iota(jnp.int32, (bm, bm), 0)
    c_i = lax.broadcasted_iota(jnp.int32, (bm, bm), 1)
    tri = jnp.where(c_i <= r_i, 1.0, 0.0).astype(BF16)
    hi, mid, lo = _split3(ls)
    cum = (jnp.dot(tri, hi, preferred_element_type=F32)
           + jnp.dot(tri, mid, preferred_element_type=F32)
           + jnp.dot(tri, lo, preferred_element_type=F32)) + carry_ref[...]
    carry_ref[...] = cum[bm - 1:bm, :]
    neg = cum * (-LOG2E)
    lane = lax.broadcasted_iota(jnp.int32, (bm, LANES), 1)
    for h in range(HEADS):
        hi, mid, lo = (t.astype(F32) for t in _split3(jnp.broadcast_to(neg[:, h:h + 1], (bm, LANES))))
        o_ref[h] = jnp.where(lane == 0, hi, jnp.where(lane == 1, mid, jnp.where(lane == 2, lo, 0.0))).astype(BF16)


def forget_cumsum(x, w_f, b_f, bm=512):
    b, s, d = x.shape
    w = jnp.pad(w_f, ((0, 0), (0, LANES - HEADS))).astype(BF16)
    bias = jnp.pad(b_f, (0, LANES - HEADS)).reshape(1, LANES).astype(F32)
    return pl.pallas_call(
        functools.partial(_fgate_kernel, bm=bm),
        grid=(b, s // bm),
        in_specs=[
            pl.BlockSpec((None, bm, d), lambda bi, i: (bi, i, 0)),
            pl.BlockSpec((d, LANES), lambda bi, i: (0, 0)),
            pl.BlockSpec((1, LANES), lambda bi, i: (0, 0)),
        ],
        out_specs=pl.BlockSpec((None, HEADS, bm, LANES), lambda bi, i: (bi, 0, i, 0)),
        out_shape=jax.ShapeDtypeStruct((b, HEADS, s, LANES), BF16),
        scratch_shapes=[pltpu.VMEM((1, LANES), F32)],
        compiler_params=_params("parallel", "arbitrary"),
        name="forget_cumsum",
    )(x, w, bias)


FOX_GROUP = 4


def _fox_kernel(q_ref, k_ref, v_ref, f_ref, o_ref, vt_ref, acc_ref, *, bq):
    qi = pl.program_id(2)
    nblk = pl.num_programs(2)
    d = HEAD_DIM
    heads = range(FOX_GROUP)
    hs = [slice(h * d, (h + 1) * d) for h in heads]

    @pl.when(qi == 0)
    def _():
        ones_row = jnp.where(lax.broadcasted_iota(jnp.int32, (ONES_PAD, bq), 0) == 0, 1.0, 0.0).astype(BF16)

        def tr(c, carry):
            cs = pl.multiple_of(c * bq, bq)
            for h in heads:
                vt_ref[h, :d, pl.ds(cs, bq)] = v_ref[pl.ds(cs, bq), hs[h]].astype(F32).T.astype(BF16)
                vt_ref[h, d:, pl.ds(cs, bq)] = ones_row
            return carry
        lax.fori_loop(0, nblk, tr, 0)

    acc_ref[...] = jnp.zeros_like(acc_ref)

    ones3 = jnp.where(lax.broadcasted_iota(jnp.int32, (bq, LANES), 1) < 3, 1.0, 0.0).astype(BF16)
    q_aug = [jnp.concatenate([q_ref[:, hs[h]], ones3], axis=1) for h in heads]

    def scores(h, kb):
        ks = pl.multiple_of(kb * bq, bq)
        k_aug = jnp.concatenate([k_ref[pl.ds(ks, bq), hs[h]], f_ref[h, pl.ds(ks, bq), :]], axis=1)
        return lax.dot_general(k_aug, q_aug[h], NT_DIMS, preferred_element_type=F32)

    def values(h, kb, p):
        ks = pl.multiple_of(kb * bq, bq)
        return jnp.dot(vt_ref[h, :, pl.ds(ks, bq)], p, preferred_element_type=F32)

    def step(kb, carry, diagonal):
        sts = [scores(h, kb) for h in heads]
        if diagonal:
            keep = (lax.broadcasted_iota(jnp.int32, (bq, bq), 0) <= lax.broadcasted_iota(jnp.int32, (bq, bq), 1))
            sts = [jnp.where(keep, st, NEG) for st in sts]
        out = []
        for h in heads:
            m = carry[h]
            m_new = jnp.maximum(m, jnp.max(sts[h], axis=0, keepdims=True))
            p = jnp.exp2(sts[h] - m_new).astype(BF16)
            acc_ref[h] = jnp.exp2(m - m_new) * acc_ref[h] + values(h, kb, p)
            out.append(m_new)
        return tuple(out)

    init = tuple(jnp.full((1, bq), NEG, F32) for _ in heads)
    carry = lax.fori_loop(0, qi, lambda kb, cr: step(kb, cr, False), init)
    step(qi, carry, True)
    for h in heads:
        o_ref[:, hs[h]] = (acc_ref[h, :d, :] / acc_ref[h, d:d + 1, :]).T.astype(o_ref.dtype)


def fox_attention(qkv, fcum, q_col, k_col, v_col, bq=512):
    b, s, _ = qkv.shape
    g = FOX_GROUP
    w = g * HEAD_DIM
    assert q_col % g == 0 and k_col % g == 0 and v_col % g == 0
    return pl.pallas_call(
        functools.partial(_fox_kernel, bq=bq),
        grid=(b, HEADS // g, s // bq),
        in_specs=[
            pl.BlockSpec((None, bq, w), lambda bi, h, i: (bi, i, q_col // g + h)),
            pl.BlockSpec((None, s, w), lambda bi, h, i: (bi, 0, k_col // g + h), pipeline_mode=pl.Buffered(1)),
            pl.BlockSpec((None, s, w), lambda bi, h, i: (bi, 0, v_col // g + h), pipeline_mode=pl.Buffered(1)),
            pl.BlockSpec((None, g, s, LANES), lambda bi, h, i: (bi, h, 0, 0), pipeline_mode=pl.Buffered(1)),
        ],
        out_specs=pl.BlockSpec((None, bq, w), lambda bi, h, i: (bi, i, h)),
        out_shape=jax.ShapeDtypeStruct((b, s, BRANCH), BF16),
        scratch_shapes=[pltpu.VMEM((g, HEAD_DIM + ONES_PAD, s), BF16), pltpu.VMEM((g, HEAD_DIM + ONES_PAD, bq), F32)],
        compiler_params=_params("parallel", "parallel", "arbitrary"),
        name="fox_attention",
    )(qkv, qkv, qkv, fcum)


CA_GROUP = 4


def _chunk_attn_kernel(q_ref, kp_ref, kc_ref, vp_ref, vc_ref, row_ref, o_ref, bias_ref, *, bq):
    qi = pl.program_id(2)
    d = HEAD_DIM
    hq = bq // 2
    heads = range(CA_GROUP)
    hs = [slice(h * d, (h + 1) * d) for h in heads]

    @pl.when((pl.program_id(1) == 0) & (qi == 0))
    def _():
        w = 2 * bq
        qc = lax.broadcasted_iota(jnp.int32, (hq, w), 0) // CHUNK
        kc = lax.broadcasted_iota(jnp.int32, (hq, w), 1) // CHUNK - bq // CHUNK
        for h in heads:
            toeplitz = pltpu.roll(jnp.broadcast_to(row_ref[h], (hq, w)), 0, 1, stride=1, stride_axis=0)
            band = jnp.where(kc <= qc, jnp.where(kc >= qc - CA_LEFT_CHUNKS, toeplitz, NEG), NEG)
            bias_ref[h] = band[:, :3 * hq].T

    row_i = lax.broadcasted_iota(jnp.int32, (3 * hq, hq), 0)
    keep = [row_i >= jnp.where(qi == 0, bq, 0), row_i >= jnp.where(qi == 0, hq, 0)]
    ones_rows = jnp.where(lax.broadcasted_iota(jnp.int32, (ONES_PAD, bq), 0) == 0, 1.0, 0.0).astype(BF16)

    sts = []
    for h in heads:
        kp, kc_ = kp_ref[:, hs[h]], kc_ref[:, hs[h]]
        windows = [jnp.concatenate([kp, kc_[:hq]], axis=0), jnp.concatenate([kp[hq:], kc_], axis=0)]
        for j in range(2):
            s = lax.dot_general(windows[j], q_ref[j * hq:(j + 1) * hq, hs[h]], NT_DIMS,
                                preferred_element_type=F32) + bias_ref[h]
            sts.append(jnp.where(keep[j], s, NEG))
    for h in heads:
        vtp = jnp.concatenate([vp_ref[:, hs[h]].astype(F32).T.astype(BF16), ones_rows], axis=0)
        vtc = jnp.concatenate([vc_ref[:, hs[h]].astype(F32).T.astype(BF16), ones_rows], axis=0)
        windows = [jnp.concatenate([vtp, vtc[:, :hq]], axis=1), jnp.concatenate([vtp[:, hq:], vtc], axis=1)]
        for j in range(2):
            s = sts[2 * h + j]
            p = jnp.exp(s - jnp.max(s, axis=0, keepdims=True)).astype(BF16)
            o = jnp.dot(windows[j], p, preferred_element_type=F32)
            o_ref[j * hq:(j + 1) * hq, hs[h]] = (o[:d] / o[d:d + 1]).T.astype(o_ref.dtype)


def _distance_rows(rel_bias, bq):
    heads, table = rel_bias.shape
    assert table == (CHUNK - 1) + REL_CLIP + 1 and bq >= REL_CLIP
    far = rel_bias[:, table - 1:]
    return jnp.concatenate([
        jnp.broadcast_to(far, (heads, bq - REL_CLIP)),
        rel_bias[:, ::-1],
        jnp.broadcast_to(far, (heads, bq - CHUNK)),
    ], axis=1).astype(F32).reshape(heads, 1, 2 * bq)


def chunk_attention(qkv, rel_bias, q_col, k_col, v_col, bq=512):
    b, s, _ = qkv.shape
    assert bq == CA_LEFT_CHUNKS * CHUNK
    rows = _distance_rows(rel_bias, bq)
    g = CA_GROUP
    w = g * HEAD_DIM
    assert q_col % g == 0 and k_col % g == 0 and v_col % g == 0
    cur = lambda col: pl.BlockSpec((None, bq, w), lambda h, bi, i: (bi, i, col // g + h))
    prev = lambda col: pl.BlockSpec((None, bq, w), lambda h, bi, i: (bi, jnp.maximum(i - 1, 0), col // g + h))
    return pl.pallas_call(
        functools.partial(_chunk_attn_kernel, bq=bq),
        grid=(HEADS // g, b, s // bq),
        in_specs=[cur(q_col), prev(k_col), cur(k_col), prev(v_col), cur(v_col),
                  pl.BlockSpec((g, 1, 2 * bq), lambda h, bi, i: (h, 0, 0))],
        out_specs=pl.BlockSpec((None, bq, w), lambda h, bi, i: (bi, i, h)),
        out_shape=jax.ShapeDtypeStruct((b, s, BRANCH), BF16),
        scratch_shapes=[pltpu.VMEM((g, 3 * bq // 2, bq // 2), F32)],
        compiler_params=_params("arbitrary", "arbitrary", "arbitrary"),
        name="chunk_attention",
    )(qkv, qkv, qkv, qkv, qkv, rows)


def _outproj_ln_kernel(a_ref, b_ref, wa_ref, wb_ref, x_ref, g_ref, beta_ref, o_ref):
    mix = (jnp.dot(a_ref[...], wa_ref[...], preferred_element_type=F32)
           + jnp.dot(b_ref[...], wb_ref[...], preferred_element_type=F32))
    o_ref[...] = _layer_norm_rows(ALPHA * x_ref[...] + mix, g_ref[...], beta_ref[...])


def outproj_ln(a, b, w_out, x, g, beta, bm=512):
    t, d = x.shape
    vec = pl.BlockSpec((1, d), lambda i: (0, 0))
    act = pl.BlockSpec((bm, BRANCH), lambda i: (i, 0))
    return pl.pallas_call(
        _outproj_ln_kernel,
        grid=(t // bm,),
        in_specs=[act, act,
                  pl.BlockSpec((BRANCH, d), lambda i: (0, 0)), pl.BlockSpec((BRANCH, d), lambda i: (1, 0)),
                  pl.BlockSpec((bm, d), lambda i: (i, 0)), vec, vec],
        out_specs=pl.BlockSpec((bm, d), lambda i: (i, 0)),
        out_shape=jax.ShapeDtypeStruct((t, d), F32),
        compiler_params=_params("parallel"),
        name="outproj_ln",
    )(a, b, w_out, w_out, x, g.reshape(1, d), beta.reshape(1, d))


def _mlp_ln_kernel(x_ref, w1_ref, w2_ref, g_ref, beta_ref, o_ref, xb_ref, acc_ref):
    j = pl.program_id(1)

    @pl.when(j == 0)
    def _():
        xb_ref[...] = x_ref[...].astype(BF16)
        acc_ref[...] = jnp.zeros_like(acc_ref)

    h = jnp.maximum(jnp.dot(xb_ref[...], w1_ref[...], preferred_element_type=F32), 0.0)
    acc_ref[...] += jnp.dot((h * h).astype(BF16), w2_ref[...], preferred_element_type=F32)

    @pl.when(j == pl.num_programs(1) - 1)
    def _():
        o_ref[...] = _layer_norm_rows(ALPHA * x_ref[...] + acc_ref[...], g_ref[...], beta_ref[...])


def mlp_ln(x, w1, w2, g, beta, bm=512, bf=1024):
    t, d = x.shape
    ff = w1.shape[1]
    vec = pl.BlockSpec((1, d), lambda i, j: (0, 0))
    return pl.pallas_call(
        _mlp_ln_kernel,
        grid=(t // bm, ff // bf),
        in_specs=[
            pl.BlockSpec((bm, d), lambda i, j: (i, 0)),
            pl.BlockSpec((d, bf), lambda i, j: (0, j)),
            pl.BlockSpec((bf, d), lambda i, j: (j, 0)),
            vec, vec,
        ],
        out_specs=pl.BlockSpec((bm, d), lambda i, j: (i, 0)),
        out_shape=jax.ShapeDtypeStruct((t, d), F32),
        scratch_shapes=[pltpu.VMEM((bm, d), BF16), pltpu.VMEM((bm, d), F32)],
        compiler_params=_params("parallel", "arbitrary"),
        name="mlp_ln",
    )(x, w1, w2, g.reshape(1, d), beta.reshape(1, d))


def even_mixer(x, w_in, conv_w, conv_b, conv_ln_g, conv_ln_b, lb_logits, slot, gnorm_g):
    b, s, d = x.shape
    n = w_in.shape[1]
    u = inproj(x.reshape(b * s, d), w_in, jnp.ones((1, n), F32), F32).reshape(b, s, n)
    a_out = conv_branch(u, conv_w, conv_b, conv_ln_g, conv_ln_b)
    b_out = hgrn2(u, lb_logits, slot, gnorm_g)
    return a_out.reshape(b * s, BRANCH), b_out.reshape(b * s, BRANCH)


def odd_mixer(x, w_in_all, layer, b_f, rel_bias):
    b, s, d = x.shape
    w_c = cast_layer_bf16(w_in_all, layer, cols=3 * BRANCH)
    w_f = w_in_all[layer][:, 3 * BRANCH:3 * BRANCH + HEADS]
    w_d = w_in_all[layer][:, 3 * BRANCH + HEADS:].astype(BF16)
    scale = HEAD_DIM ** -0.5
    ones = jnp.ones((1, BRANCH), F32)
    x2 = x.reshape(b * s, d)
    qkv_c = inproj(x2, w_c, jnp.concatenate([ones * (scale * LOG2E), ones, ones], axis=1), BF16)
    qkv_d = inproj(x2, w_d, jnp.concatenate([ones * scale, ones, ones], axis=1), BF16)
    fcum = forget_cumsum(x, w_f, b_f)
    c_out = fox_attention(qkv_c.reshape(b, s, 3 * BRANCH), fcum, 0, HEADS, 2 * HEADS)
    d_out = chunk_attention(qkv_d.reshape(b, s, 3 * BRANCH), rel_bias, 0, HEADS, 2 * HEADS)
    return c_out.reshape(b * s, BRANCH), d_out.reshape(b * s, BRANCH)


def kernel(x, ev_w_in, ev_conv_w, ev_conv_b, ev_conv_ln_g, ev_conv_ln_b, hgrn_lb_logits, ev_gnorm_g, ev_w_out, od_w_in, fox_b_f, rel_bias, od_w_out, ln_mix_g, ln_mix_b, mlp_w1, mlp_w2, ln_mlp_g, ln_mlp_b):
    b, s, d = x.shape
    for l in range(DEPTH):
        j = l // 2
        if l % 2 == 0:
            p, r = even_mixer(x, cast_layer_bf16(ev_w_in, j), ev_conv_w[j], ev_conv_b[j], ev_conv_ln_g[j],
                              ev_conv_ln_b[j], hgrn_lb_logits, j, ev_gnorm_g[j])
            w_out = cast_layer_bf16(ev_w_out, j)
        else:
            p, r = odd_mixer(x, od_w_in, j, fox_b_f[j], rel_bias[j])
            w_out = cast_layer_bf16(od_w_out, j)
        x2 = outproj_ln(p, r, w_out, x.reshape(b * s, d), ln_mix_g[l], ln_mix_b[l])
        x2 = mlp_ln(x2, cast_layer_bf16(mlp_w1, l), cast_layer_bf16(mlp_w2, l), ln_mlp_g[l], ln_mlp_b[l])
        x = x2.reshape(b, s, d)
    return x
```

```python
import functools

import jax
import jax.numpy as jnp
from jax import lax
from jax.experimental import pallas as pl
from jax.experimental.pallas import tpu as pltpu

F32 = jnp.float32
BF16 = jnp.bfloat16

D_MODEL = 2048
DEPTH = 2
CHUNK = 64
LN_EPS = 1e-5
ALPHA = (2 * DEPTH) ** 0.25
HEADS = 8
HEAD_DIM = 128
BRANCH = HEADS * HEAD_DIM
CONV_WIDTH = 31
CONV_HALO = 32
CA_LEFT_CHUNKS = 8
REL_CLIP = 256
D_FF = 4 * D_MODEL
NEG = -1e30
LOG2E = 1.4426950408889634

ONES_PAD = 16
LANES = 128
SUBLANES = 8
VMEM_LIMIT = 56 * 1024 * 1024

NT_DIMS = (((1,), (1,)), ((), ()))
TN_DIMS = (((0,), (0,)), ((), ()))


def _params(*sem):
    return pltpu.CompilerParams(dimension_semantics=sem, vmem_limit_bytes=VMEM_LIMIT)


def _sigmoid(x):
    return 1.0 / (1.0 + jnp.exp(-x))


def _layer_norm_rows(y, g, b):
    mu = jnp.mean(y, axis=-1, keepdims=True)
    d = y - mu
    var = jnp.mean(d * d, axis=-1, keepdims=True)
    return d * lax.rsqrt(var + LN_EPS) * g + b


def _split3(x):
    hi = x.astype(BF16)
    r = x - hi.astype(F32)
    mid = r.astype(BF16)
    lo = (r - mid.astype(F32)).astype(BF16)
    return hi, mid, lo


def _cast_kernel(w_ref, o_ref):
    o_ref[...] = w_ref[...].astype(o_ref.dtype)


def cast_layer_bf16(w, layer, br=512, bc=2048):
    _, r, c = w.shape
    br, bc = min(br, r), min(bc, c)
    return pl.pallas_call(
        _cast_kernel,
        grid=(r // br, c // bc),
        in_specs=[pl.BlockSpec((None, br, bc), lambda i, j: (layer, i, j))],
        out_specs=pl.BlockSpec((br, bc), lambda i, j: (i, j)),
        out_shape=jax.ShapeDtypeStruct((r, c), BF16),
        compiler_params=_params("parallel", "parallel"),
        name="cast_bf16",
    )(w)


def _inproj_kernel(x_ref, w_ref, s_ref, o_ref, xb_ref):
    @pl.when(pl.program_id(1) == 0)
    def _():
        xb_ref[...] = x_ref[...].astype(BF16)

    acc = jnp.dot(xb_ref[...], w_ref[...], preferred_element_type=F32)
    o_ref[...] = (acc * s_ref[...]).astype(o_ref.dtype)


def inproj(x, w, colscale, out_dtype, bm=1024, bn=1024):
    t, k = x.shape
    n = w.shape[1]
    return pl.pallas_call(
        _inproj_kernel,
        grid=(t // bm, n // bn),
        in_specs=[
            pl.BlockSpec((bm, k), lambda i, j: (i, 0)),
            pl.BlockSpec((k, bn), lambda i, j: (0, j)),
            pl.BlockSpec((1, bn), lambda i, j: (0, j)),
        ],
        out_specs=pl.BlockSpec((bm, bn), lambda i, j: (i, j)),
        out_shape=jax.ShapeDtypeStruct((t, n), out_dtype),
        scratch_shapes=[pltpu.VMEM((bm, k), BF16)],
        compiler_params=_params("parallel", "arbitrary"),
        name="inproj",
    )(x, w, colscale)


CONV_ROWS = 64


def _conv_kernel(a_ref, g_ref, ah_ref, gh_ref, w_ref, cb_ref, lg_ref, lb_ref, o_ref, hext_ref, hs_ref, y_ref, wb_ref,
                 *, bt):
    i = pl.program_id(1)
    halo = ah_ref[...] * _sigmoid(gh_ref[...])
    hext_ref[0:CONV_HALO, :] = jnp.where(i > 0, halo, 0.0)
    hext_ref[CONV_HALO:CONV_HALO + bt, :] = a_ref[...] * _sigmoid(g_ref[...])
    span = bt + CONV_HALO - SUBLANES
    for d in range(1, SUBLANES):
        hs_ref[d - 1, 0:span, :] = hext_ref[d:d + span, :]

    first = CONV_HALO - (CONV_WIDTH - 1)
    for j in range(CONV_WIDTH):
        wb_ref[j] = jnp.broadcast_to(w_ref[j:j + 1, :], (SUBLANES, BRANCH))
    wb_ref[CONV_WIDTH] = jnp.broadcast_to(cb_ref[...], (SUBLANES, BRANCH))
    groups = CONV_ROWS // SUBLANES

    def rows(c, carry):
        base = pl.multiple_of(c * CONV_ROWS, CONV_ROWS)
        for cg in range(BRANCH // 256):
            cs = slice(cg * 256, (cg + 1) * 256)
            acc = jnp.broadcast_to(wb_ref[CONV_WIDTH, :, cs][None], (groups, SUBLANES, 256))
            for j in range(CONV_WIDTH):
                off = first + j
                d, al = off % SUBLANES, off - off % SUBLANES
                if d == 0:
                    tap = hext_ref[pl.ds(base + al, CONV_ROWS), cs]
                else:
                    tap = hs_ref[d - 1, pl.ds(base + al, CONV_ROWS), cs]
                acc = acc + wb_ref[j, :, cs][None] * tap.reshape(groups, SUBLANES, 256)
            y_ref[pl.ds(base, CONV_ROWS), cs] = acc.reshape(CONV_ROWS, 256)
        return carry

    lax.fori_loop(0, bt // CONV_ROWS, rows, 0)
    z = _layer_norm_rows(y_ref[...], lg_ref[...], lb_ref[...])
    o_ref[...] = (z * _sigmoid(z)).astype(o_ref.dtype)


def conv_branch(u, conv_w, conv_b, ln_g, ln_b, bt=512):
    b, s, _ = u.shape
    hb = bt // CONV_HALO
    w = jnp.pad(conv_w, ((0, CONV_HALO - CONV_WIDTH), (0, 0)))
    row = lambda v: v.reshape(1, BRANCH)
    vec = pl.BlockSpec((1, BRANCH), lambda bi, i: (0, 0))
    return pl.pallas_call(
        functools.partial(_conv_kernel, bt=bt),
        grid=(b, s // bt),
        in_specs=[
            pl.BlockSpec((None, bt, BRANCH), lambda bi, i: (bi, i, 0)),
            pl.BlockSpec((None, bt, BRANCH), lambda bi, i: (bi, i, 1)),
            pl.BlockSpec((None, CONV_HALO, BRANCH), lambda bi, i: (bi, jnp.maximum(i * hb - 1, 0), 0)),
            pl.BlockSpec((None, CONV_HALO, BRANCH), lambda bi, i: (bi, jnp.maximum(i * hb - 1, 0), 1)),
            pl.BlockSpec((CONV_HALO, BRANCH), lambda bi, i: (0, 0)),
            vec, vec, vec,
        ],
        out_specs=pl.BlockSpec((None, bt, BRANCH), lambda bi, i: (bi, i, 0)),
        out_shape=jax.ShapeDtypeStruct((b, s, BRANCH), BF16),
        scratch_shapes=[
            pltpu.VMEM((bt + CONV_HALO, BRANCH), F32),
            pltpu.VMEM((SUBLANES - 1, bt + CONV_HALO - SUBLANES, BRANCH), F32),
            pltpu.VMEM((bt, BRANCH), F32),
            pltpu.VMEM((CONV_WIDTH + 1, SUBLANES, BRANCH), F32),
        ],
        compiler_params=_params("parallel", "arbitrary"),
        name="conv_branch",
    )(u, u, u, u, w, row(conv_b), row(ln_g), row(ln_b))


HG_HALF = CHUNK // 2
HG_PAIR = 2 * HEAD_DIM
HG_STEPS = 4


def _hgrn_kernel(q_ref, f_ref, v_ref, g_ref, lbl_ref, gn_ref, o_ref, state_ref, *, slot):
    @pl.when(pl.program_id(1) == 0)
    def _():
        state_ref[...] = jnp.zeros_like(state_ref)

    c, hh = CHUNK, HG_HALF
    lbl = lbl_ref[...]
    e = jnp.exp(lbl - jnp.max(lbl, axis=0, keepdims=True))
    upto = lax.broadcasted_iota(jnp.int32, e.shape, 0) <= slot
    lb = jnp.sum(jnp.where(upto, e, 0.0), axis=0, keepdims=True) / jnp.sum(e, axis=0, keepdims=True)
    r_i = lax.broadcasted_iota(jnp.int32, (c, c), 0)
    c_i = lax.broadcasted_iota(jnp.int32, (c, c), 1)
    tri = jnp.where(c_i <= r_i, 1.0, 0.0).astype(BF16)

    def gates(rows):
        f = lb + (1.0 - lb) * _sigmoid(f_ref[rows, :])
        lf = jnp.log(f)
        kk = 1.0 - f
        q = q_ref[rows, :]
        qs = q * _sigmoid(q)
        l3 = jnp.dot(tri, jnp.concatenate(_split3(lf), axis=1), preferred_element_type=F32)
        ll = l3[:, :BRANCH] + l3[:, BRANCH:2 * BRANCH] + l3[:, 2 * BRANCH:]
        l_mid0 = ll[hh // 2 - 1:hh // 2]
        l_edge = ll[hh - 1:hh]
        l_mid1 = ll[hh + hh // 2 - 1:hh + hh // 2]
        l_end = ll[c - 1:c]
        lt, lbt = ll[:hh], ll[hh:]
        return dict(
            a_d0=qs[:hh] * jnp.exp(lt - l_mid0),
            b_d0=kk[:hh] * jnp.exp(l_mid0 - lt),
            a_off=qs[hh:] * jnp.exp(lbt - l_edge),
            b_off=kk[:hh] * jnp.exp(l_edge - lt),
            a_d1=qs[hh:] * jnp.exp(lbt - l_mid1),
            b_d1=kk[hh:] * jnp.exp(l_mid1 - lbt),
            a_int=(qs * jnp.exp(ll)).astype(BF16),
            kd=(kk * jnp.exp(l_end - ll)).astype(BF16),
            dec=jnp.exp(l_end),
        )

    zero = jnp.zeros((hh, HG_PAIR), F32)
    first = lax.broadcasted_iota(jnp.int32, (c, HG_PAIR), 1) < HEAD_DIM
    first3 = jnp.concatenate([first, first, first], axis=1)
    row2 = lax.broadcasted_iota(jnp.int32, (HG_PAIR, HG_PAIR), 0) < HEAD_DIM
    col2 = lax.broadcasted_iota(jnp.int32, (HG_PAIR, HG_PAIR), 1) < HEAD_DIM
    same_head = row2 == col2
    key_i = lax.broadcasted_iota(jnp.int32, (c, 2 * c), 1) % c
    qry_i = lax.broadcasted_iota(jnp.int32, (c, 2 * c), 0)
    stack = lambda top, bottom: jnp.concatenate([top, bottom], axis=0)

    pairs = range(HEADS // 2)
    pss = [slice(p * HG_PAIR, (p + 1) * HG_PAIR) for p in pairs]

    def scores(gt):
        scs = []
        for ps in pss:
            lhs = jnp.concatenate([stack(gt["a_d0"][:, ps], zero), stack(zero, gt["a_off"][:, ps]),
                                   stack(zero, gt["a_d1"][:, ps])], axis=1)
            keys = jnp.concatenate([stack(gt["b_d0"][:, ps], zero), stack(gt["b_off"][:, ps], zero),
                                    stack(zero, gt["b_d1"][:, ps])], axis=1)
            rhs_t = stack(jnp.where(first3, keys, 0.0), jnp.where(first3, 0.0, keys))
            sc = lax.dot_general(lhs.astype(BF16), rhs_t.astype(BF16), NT_DIMS, preferred_element_type=F32)
            scs.append(jnp.where(key_i <= qry_i, sc, 0.0))
        return scs

    chunks = [slice(k * c, (k + 1) * c) for k in range(o_ref.shape[0] // c)]
    gts = [gates(rows) for rows in chunks]
    scss = [scores(gt) for gt in gts]
    states = [state_ref[p] for p in pairs]
    for rows, gt, scs in zip(chunks, gts, scss):
        inter = []
        for p in pairs:
            st = states[p]
            inter.append(lax.dot_general(gt["a_int"][:, pss[p]], st.astype(BF16), NT_DIMS, preferred_element_type=F32))
            upd = lax.dot_general(v_ref[rows, pss[p]].astype(BF16), gt["kd"][:, pss[p]], TN_DIMS,
                                  preferred_element_type=F32)
            states[p] = gt["dec"][:, pss[p]] * st + jnp.where(same_head, upd, 0.0)
        for p in pairs:
            v = v_ref[rows, pss[p]]
            v_bd = stack(jnp.where(first, v, 0.0), jnp.where(first, 0.0, v)).astype(BF16)
            o = jnp.dot(scs[p].astype(BF16), v_bd, preferred_element_type=F32) + inter[p]
            for j in range(2):
                hsl = slice(p * HG_PAIR + j * HEAD_DIM, p * HG_PAIR + (j + 1) * HEAD_DIM)
                oj = o[:, j * HEAD_DIM:(j + 1) * HEAD_DIM]
                oj = oj * lax.rsqrt(jnp.mean(oj * oj, axis=-1, keepdims=True) + LN_EPS)
                g = g_ref[rows, hsl]
                o_ref[rows, hsl] = (oj * gn_ref[:, hsl] * (g * _sigmoid(g))).astype(o_ref.dtype)
    for p in pairs:
        state_ref[p] = states[p]


def hgrn2(u, lb_logits, slot, gnorm_g):
    b, s, _ = u.shape
    slots = lb_logits.shape[0]
    rows = HG_STEPS * CHUNK
    col = lambda j: pl.BlockSpec((None, rows, BRANCH), lambda bi, i: (bi, i, j))
    vec = pl.BlockSpec((1, BRANCH), lambda bi, i: (0, 0))
    return pl.pallas_call(
        functools.partial(_hgrn_kernel, slot=slot),
        grid=(b, s // rows),
        in_specs=[col(2), col(3), col(4), col(5), pl.BlockSpec((slots, BRANCH), lambda bi, i: (0, 0)), vec],
        out_specs=pl.BlockSpec((None, rows, BRANCH), lambda bi, i: (bi, i, 0)),
        out_shape=jax.ShapeDtypeStruct((b, s, BRANCH), BF16),
        scratch_shapes=[pltpu.VMEM((HEADS // 2, HG_PAIR, HG_PAIR), F32)],
        compiler_params=_params("parallel", "arbitrary"),
        name="hgrn2",
    )(u, u, u, u, lb_logits.astype(F32), gnorm_g.reshape(1, BRANCH))


def _fgate_kernel(x_ref, w_ref, b_ref, o_ref, carry_ref, *, bm):
    @pl.when(pl.program_id(1) == 0)
    def _():
        carry_ref[...] = jnp.zeros_like(carry_ref)

    z = jnp.dot(x_ref[...].astype(BF16), w_ref[...], preferred_element_type=F32) + b_ref[...]
    ls = jnp.minimum(z, 0.0) - jnp.log(1.0 + jnp.exp(-jnp.abs(z)))
    r_i = lax.broadcasted_iota(jnp.int32, (bm, bm), 0)
    c_i = lax.broadcasted_iota(jnp.int32, (bm, bm), 1)
    tri = jnp.where(c_i <= r_i, 1.0, 0.0).astype(BF16)
    hi, mid, lo = _split3(ls)
    cum = (jnp.dot(tri, hi, preferred_element_type=F32)
           + jnp.dot(tri, mid, preferred_element_type=F32)
           + jnp.dot(tri, lo, preferred_element_type=F32)) + carry_ref[...]
    carry_ref[...] = cum[bm - 1:bm, :]
    neg = cum * (-LOG2E)
    lane = lax.broadcasted_iota(jnp.int32, (bm, LANES), 1)
    for h in range(HEADS):
        hi, mid, lo = (t.astype(F32) for t in _split3(jnp.broadcast_to(neg[:, h:h + 1], (bm, LANES))))
        o_ref[h] = jnp.where(lane == 0, hi, jnp.where(lane == 1, mid, jnp.where(lane == 2, lo, 0.0))).astype(BF16)


def forget_cumsum(x, w_f, b_f, bm=512):
    b, s, d = x.shape
    w = jnp.pad(w_f, ((0, 0), (0, LANES - HEADS))).astype(BF16)
    bias = jnp.pad(b_f, (0, LANES - HEADS)).reshape(1, LANES).astype(F32)
    return pl.pallas_call(
        functools.partial(_fgate_kernel, bm=bm),
        grid=(b, s // bm),
        in_specs=[
            pl.BlockSpec((None, bm, d), lambda bi, i: (bi, i, 0)),
            pl.BlockSpec((d, LANES), lambda bi, i: (0, 0)),
            pl.BlockSpec((1, LANES), lambda bi, i: (0, 0)),
        ],
        out_specs=pl.BlockSpec((None, HEADS, bm, LANES), lambda bi, i: (bi, 0, i, 0)),
        out_shape=jax.ShapeDtypeStruct((b, HEADS, s, LANES), BF16),
        scratch_shapes=[pltpu.VMEM((1, LANES), F32)],
        compiler_params=_params("parallel", "arbitrary"),
        name="forget_cumsum",
    )(x, w, bias)


FOX_GROUP = 4


def _fox_kernel(q_ref, k_ref, v_ref, f_ref, o_ref, vt_ref, acc_ref, *, bq):
    qi = pl.program_id(2)
    nblk = pl.num_programs(2)
    d = HEAD_DIM
    heads = range(FOX_GROUP)
    hs = [slice(h * d, (h + 1) * d) for h in heads]

    @pl.when(qi == 0)
    def _():
        ones_row = jnp.where(lax.broadcasted_iota(jnp.int32, (ONES_PAD, bq), 0) == 0, 1.0, 0.0).astype(BF16)

        def tr(c, carry):
            cs = pl.multiple_of(c * bq, bq)
            for h in heads:
                vt_ref[h, :d, pl.ds(cs, bq)] = v_ref[pl.ds(cs, bq), hs[h]].astype(F32).T.astype(BF16)
                vt_ref[h, d:, pl.ds(cs, bq)] = ones_row
            return carry
        lax.fori_loop(0, nblk, tr, 0)

    acc_ref[...] = jnp.zeros_like(acc_ref)

    ones3 = jnp.where(lax.broadcasted_iota(jnp.int32, (bq, LANES), 1) < 3, 1.0, 0.0).astype(BF16)
    q_aug = [jnp.concatenate([q_ref[:, hs[h]], ones3], axis=1) for h in heads]

    def scores(h, kb):
        ks = pl.multiple_of(kb * bq, bq)
        k_aug = jnp.concatenate([k_ref[pl.ds(ks, bq), hs[h]], f_ref[h, pl.ds(ks, bq), :]], axis=1)
        return lax.dot_general(k_aug, q_aug[h], NT_DIMS, preferred_element_type=F32)

    def values(h, kb, p):
        ks = pl.multiple_of(kb * bq, bq)
        return jnp.dot(vt_ref[h, :, pl.ds(ks, bq)], p, preferred_element_type=F32)

    def step(kb, carry, diagonal):
        sts = [scores(h, kb) for h in heads]
        if diagonal:
            keep = (lax.broadcasted_iota(jnp.int32, (bq, bq), 0) <= lax.broadcasted_iota(jnp.int32, (bq, bq), 1))
            sts = [jnp.where(keep, st, NEG) for st in sts]
        out = []
        for h in heads:
            m = carry[h]
            m_new = jnp.maximum(m, jnp.max(sts[h], axis=0, keepdims=True))
            p = jnp.exp2(sts[h] - m_new).astype(BF16)
            acc_ref[h] = jnp.exp2(m - m_new) * acc_ref[h] + values(h, kb, p)
            out.append(m_new)
        return tuple(out)

    init = tuple(jnp.full((1, bq), NEG, F32) for _ in heads)
    carry = lax.fori_loop(0, qi, lambda kb, cr: step(kb, cr, False), init)
    step(qi, carry, True)
    for h in heads:
        o_ref[:, hs[h]] = (acc_ref[h, :d, :] / acc_ref[h, d:d + 1, :]).T.astype(o_ref.dtype)


def fox_attention(qkv, fcum, q_col, k_col, v_col, bq=512):
    b, s, _ = qkv.shape
    g = FOX_GROUP
    w = g * HEAD_DIM
    assert q_col % g == 0 and k_col % g == 0 and v_col % g == 0
    return pl.pallas_call(
        functools.partial(_fox_kernel, bq=bq),
        grid=(b, HEADS // g, s // bq),
        in_specs=[
            pl.BlockSpec((None, bq, w), lambda bi, h, i: (bi, i, q_col // g + h)),
            pl.BlockSpec((None, s, w), lambda bi, h, i: (bi, 0, k_col // g + h), pipeline_mode=pl.Buffered(1)),
            pl.BlockSpec((None, s, w), lambda bi, h, i: (bi, 0, v_col // g + h), pipeline_mode=pl.Buffered(1)),
            pl.BlockSpec((None, g, s, LANES), lambda bi, h, i: (bi, h, 0, 0), pipeline_mode=pl.Buffered(1)),
        ],
        out_specs=pl.BlockSpec((None, bq, w), lambda bi, h, i: (bi, i, h)),
        out_shape=jax.ShapeDtypeStruct((b, s, BRANCH), BF16),
        scratch_shapes=[pltpu.VMEM((g, HEAD_DIM + ONES_PAD, s), BF16), pltpu.VMEM((g, HEAD_DIM + ONES_PAD, bq), F32)],
        compiler_params=_params("parallel", "parallel", "arbitrary"),
        name="fox_attention",
    )(qkv, qkv, qkv, fcum)


CA_GROUP = 4


def _chunk_attn_kernel(q_ref, kp_ref, kc_ref, vp_ref, vc_ref, row_ref, o_ref, bias_ref, *, bq):
    qi = pl.program_id(2)
    d = HEAD_DIM
    hq = bq // 2
    heads = range(CA_GROUP)
    hs = [slice(h * d, (h + 1) * d) for h in heads]

    @pl.when((pl.program_id(1) == 0) & (qi == 0))
    def _():
        w = 2 * bq
        qc = lax.broadcasted_iota(jnp.int32, (hq, w), 0) // CHUNK
        kc = lax.broadcasted_iota(jnp.int32, (hq, w), 1) // CHUNK - bq // CHUNK
        for h in heads:
            toeplitz = pltpu.roll(jnp.broadcast_to(row_ref[h], (hq, w)), 0, 1, stride=1, stride_axis=0)
            band = jnp.where(kc <= qc, jnp.where(kc >= qc - CA_LEFT_CHUNKS, toeplitz, NEG), NEG)
            bias_ref[h] = band[:, :3 * hq].T

    row_i = lax.broadcasted_iota(jnp.int32, (3 * hq, hq), 0)
    keep = [row_i >= jnp.where(qi == 0, bq, 0), row_i >= jnp.where(qi == 0, hq, 0)]
    ones_rows = jnp.where(lax.broadcasted_iota(jnp.int32, (ONES_PAD, bq), 0) == 0, 1.0, 0.0).astype(BF16)

    sts = []
    for h in heads:
        kp, kc_ = kp_ref[:, hs[h]], kc_ref[:, hs[h]]
        windows = [jnp.concatenate([kp, kc_[:hq]], axis=0), jnp.concatenate([kp[hq:], kc_], axis=0)]
        for j in range(2):
            s = lax.dot_general(windows[j], q_ref[j * hq:(j + 1) * hq, hs[h]], NT_DIMS,
                                preferred_element_type=F32) + bias_ref[h]
            sts.append(jnp.where(keep[j], s, NEG))
    for h in heads:
        vtp = jnp.concatenate([vp_ref[:, hs[h]].astype(F32).T.astype(BF16), ones_rows], axis=0)
        vtc = jnp.concatenate([vc_ref[:, hs[h]].astype(F32).T.astype(BF16), ones_rows], axis=0)
        windows = [jnp.concatenate([vtp, vtc[:, :hq]], axis=1), jnp.concatenate([vtp[:, hq:], vtc], axis=1)]
        for j in range(2):
            s = sts[2 * h + j]
            p = jnp.exp(s - jnp.max(s, axis=0, keepdims=True)).astype(BF16)
            o = jnp.dot(windows[j], p, preferred_element_type=F32)
            o_ref[j * hq:(j + 1) * hq, hs[h]] = (o[:d] / o[d:d + 1]).T.astype(o_ref.dtype)


def _distance_rows(rel_bias, bq):
    heads, table = rel_bias.shape
    assert table == (CHUNK - 1) + REL_CLIP + 1 and bq >= REL_CLIP
    far = rel_bias[:, table - 1:]
    return jnp.concatenate([
        jnp.broadcast_to(far, (heads, bq - REL_CLIP)),
        rel_bias[:, ::-1],
        jnp.broadcast_to(far, (heads, bq - CHUNK)),
    ], axis=1).astype(F32).reshape(heads, 1, 2 * bq)


def chunk_attention(qkv, rel_bias, q_col, k_col, v_col, bq=512):
    b, s, _ = qkv.shape
    assert bq == CA_LEFT_CHUNKS * CHUNK
    rows = _distance_rows(rel_bias, bq)
    g = CA_GROUP
    w = g * HEAD_DIM
    assert q_col % g == 0 and k_col % g == 0 and v_col % g == 0
    cur = lambda col: pl.BlockSpec((None, bq, w), lambda h, bi, i: (bi, i, col // g + h))
    prev = lambda col: pl.BlockSpec((None, bq, w), lambda h, bi, i: (bi, jnp.maximum(i - 1, 0), col // g + h))
    return pl.pallas_call(
        functools.partial(_chunk_attn_kernel, bq=bq),
        grid=(HEADS // g, b, s // bq),
        in_specs=[cur(q_col), prev(k_col), cur(k_col), prev(v_col), cur(v_col),
                  pl.BlockSpec((g, 1, 2 * bq), lambda h, bi, i: (h, 0, 0))],
        out_specs=pl.BlockSpec((None, bq, w), lambda h, bi, i: (bi, i, h)),
        out_shape=jax.ShapeDtypeStruct((b, s, BRANCH), BF16),
        scratch_shapes=[pltpu.VMEM((g, 3 * bq // 2, bq // 2), F32)],
        compiler_params=_params("arbitrary", "arbitrary", "arbitrary"),
        name="chunk_attention",
    )(qkv, qkv, qkv, qkv, qkv, rows)


def _outproj_ln_kernel(a_ref, b_ref, wa_ref, wb_ref, x_ref, g_ref, beta_ref, o_ref):
    mix = (jnp.dot(a_ref[...], wa_ref[...], preferred_element_type=F32)
           + jnp.dot(b_ref[...], wb_ref[...], preferred_element_type=F32))
    o_ref[...] = _layer_norm_rows(ALPHA * x_ref[...] + mix, g_ref[...], beta_ref[...])


def outproj_ln(a, b, w_out, x, g, beta, bm=512):
    t, d = x.shape
    vec = pl.BlockSpec((1, d), lambda i: (0, 0))
    act = pl.BlockSpec((bm, BRANCH), lambda i: (i, 0))
    return pl.pallas_call(
        _outproj_ln_kernel,
        grid=(t // bm,),
        in_specs=[act, act,
                  pl.BlockSpec((BRANCH, d), lambda i: (0, 0)), pl.BlockSpec((BRANCH, d), lambda i: (1, 0)),
                  pl.BlockSpec((bm, d), lambda i: (i, 0)), vec, vec],
        out_specs=pl.BlockSpec((bm, d), lambda i: (i, 0)),
        out_shape=jax.ShapeDtypeStruct((t, d), F32),
        compiler_params=_params("parallel"),
        name="outproj_ln",
    )(a, b, w_out, w_out, x, g.reshape(1, d), beta.reshape(1, d))


def _mlp_ln_kernel(x_ref, w1_ref, w2_ref, g_ref, beta_ref, o_ref, xb_ref, acc_ref):
    j = pl.program_id(1)

    @pl.when(j == 0)
    def _():
        xb_ref[...] = x_ref[...].astype(BF16)
        acc_ref[...] = jnp.zeros_like(acc_ref)

    h = jnp.maximum(jnp.dot(xb_ref[...], w1_ref[...], preferred_element_type=F32), 0.0)
    acc_ref[...] += jnp.dot((h * h).astype(BF16), w2_ref[...], preferred_element_type=F32)

    @pl.when(j == pl.num_programs(1) - 1)
    def _():
        o_ref[...] = _layer_norm_rows(ALPHA * x_ref[...] + acc_ref[...], g_ref[...], beta_ref[...])


def mlp_ln(x, w1, w2, g, beta, bm=512, bf=1024):
    t, d = x.shape
    ff = w1.shape[1]
    vec = pl.BlockSpec((1, d), lambda i, j: (0, 0))
    return pl.pallas_call(
        _mlp_ln_kernel,
        grid=(t // bm, ff // bf),
        in_specs=[
            pl.BlockSpec((bm, d), lambda i, j: (i, 0)),
            pl.BlockSpec((d, bf), lambda i, j: (0, j)),
            pl.BlockSpec((bf, d), lambda i, j: (j, 0)),
            vec, vec,
        ],
        out_specs=pl.BlockSpec((bm, d), lambda i, j: (i, 0)),
        out_shape=jax.ShapeDtypeStruct((t, d), F32),
        scratch_shapes=[pltpu.VMEM((bm, d), BF16), pltpu.VMEM((bm, d), F32)],
        compiler_params=_params("parallel", "arbitrary"),
        name="mlp_ln",
    )(x, w1, w2, g.reshape(1, d), beta.reshape(1, d))


def even_mixer(x, w_in, conv_w, conv_b, conv_ln_g, conv_ln_b, lb_logits, slot, gnorm_g):
    b, s, d = x.shape
    n = w_in.shape[1]
    u = inproj(x.reshape(b * s, d), w_in, jnp.ones((1, n), F32), F32).reshape(b, s, n)
    a_out = conv_branch(u, conv_w, conv_b, conv_ln_g, conv_ln_b)
    b_out = hgrn2(u, lb_logits, slot, gnorm_g)
    return a_out.reshape(b * s, BRANCH), b_out.reshape(b * s, BRANCH)


def odd_mixer(x, w_in, b_f, rel_bias):
    b, s, d = x.shape
    w_qkv = jnp.concatenate([w_in[:, :3 * BRANCH], w_in[:, 3 * BRANCH + HEADS:]], axis=1).astype(BF16)
    w_f = w_in[:, 3 * BRANCH:3 * BRANCH + HEADS]
    scale = HEAD_DIM ** -0.5
    ones = jnp.ones((BRANCH,), F32)
    colscale = jnp.concatenate([ones * (scale * LOG2E), ones, ones, ones * scale, ones, ones]).reshape(1, 6 * BRANCH)
    qkv = inproj(x.reshape(b * s, d), w_qkv, colscale, BF16).reshape(b, s, 6 * BRANCH)
    fcum = forget_cumsum(x, w_f, b_f)
    c_out = fox_attention(qkv, fcum, 0, HEADS, 2 * HEADS)
    d_out = chunk_attention(qkv, rel_bias, 3 * HEADS, 4 * HEADS, 5 * HEADS)
    return c_out.reshape(b * s, BRANCH), d_out.reshape(b * s, BRANCH)


def kernel(x, ev_w_in, ev_conv_w, ev_conv_b, ev_conv_ln_g, ev_conv_ln_b, hgrn_lb_logits, ev_gnorm_g, ev_w_out, od_w_in, fox_b_f, rel_bias, od_w_out, ln_mix_g, ln_mix_b, mlp_w1, mlp_w2, ln_mlp_g, ln_mlp_b):
    b, s, d = x.shape
    for l in range(DEPTH):
        j = l // 2
        if l % 2 == 0:
            p, r = even_mixer(x, cast_layer_bf16(ev_w_in, j), ev_conv_w[j], ev_conv_b[j], ev_conv_ln_g[j],
                              ev_conv_ln_b[j], hgrn_lb_logits, j, ev_gnorm_g[j])
            w_out = cast_layer_bf16(ev_w_out, j)
        else:
            p, r = odd_mixer(x, od_w_in[j], fox_b_f[j], rel_bias[j])
            w_out = cast_layer_bf16(od_w_out, j)
        x2 = outproj_ln(p, r, w_out, x.reshape(b * s, d), ln_mix_g[l], ln_mix_b[l])
        x2 = mlp_ln(x2, cast_layer_bf16(mlp_w1, l), cast_layer_bf16(mlp_w2, l), ln_mlp_g[l], ln_mlp_b[l])
        x = x2.reshape(b, s, d)
    return x
```

```python
import functools

import jax
import jax.numpy as jnp
from jax import lax
from jax.experimental import pallas as pl
from jax.experimental.pallas import tpu as pltpu

F32 = jnp.float32
BF16 = jnp.bfloat16

D_MODEL = 2048
DEPTH = 2
CHUNK = 64
LN_EPS = 1e-5
ALPHA = (2 * DEPTH) ** 0.25
HEADS = 8
HEAD_DIM = 128
BRANCH = HEADS * HEAD_DIM
CONV_WIDTH = 31
CONV_HALO = 32
CA_LEFT_CHUNKS = 8
REL_CLIP = 256
D_FF = 4 * D_MODEL
NEG = -1e30
LOG2E = 1.4426950408889634

ONES_PAD = 16
LANES = 128
SUBLANES = 8
VMEM_LIMIT = 56 * 1024 * 1024

NT_DIMS = (((1,), (1,)), ((), ()))
TN_DIMS = (((0,), (0,)), ((), ()))


def _params(*sem):
    return pltpu.CompilerParams(dimension_semantics=sem, vmem_limit_bytes=VMEM_LIMIT)


def _sigmoid(x):
    return 1.0 / (1.0 + jnp.exp(-x))


def _layer_norm_rows(y, g, b):
    mu = jnp.mean(y, axis=-1, keepdims=True)
    d = y - mu
    var = jnp.mean(d * d, axis=-1, keepdims=True)
    return d * lax.rsqrt(var + LN_EPS) * g + b


def _split3(x):
    hi = x.astype(BF16)
    r = x - hi.astype(F32)
    mid = r.astype(BF16)
    lo = (r - mid.astype(F32)).astype(BF16)
    return hi, mid, lo


def _cast_kernel(w_ref, o_ref):
    o_ref[...] = w_ref[...].astype(o_ref.dtype)


def cast_layer_bf16(w, layer, br=512, bc=2048):
    _, r, c = w.shape
    br, bc = min(br, r), min(bc, c)
    return pl.pallas_call(
        _cast_kernel,
        grid=(r // br, c // bc),
        in_specs=[pl.BlockSpec((None, br, bc), lambda i, j: (layer, i, j))],
        out_specs=pl.BlockSpec((br, bc), lambda i, j: (i, j)),
        out_shape=jax.ShapeDtypeStruct((r, c), BF16),
        compiler_params=_params("parallel", "parallel"),
        name="cast_bf16",
    )(w)


def _inproj_kernel(x_ref, w_ref, s_ref, o_ref, xb_ref):
    @pl.when(pl.program_id(1) == 0)
    def _():
        xb_ref[...] = x_ref[...].astype(BF16)

    acc = jnp.dot(xb_ref[...], w_ref[...], preferred_element_type=F32)
    o_ref[...] = (acc * s_ref[...]).astype(o_ref.dtype)


def inproj(x, w, colscale, out_dtype, bm=1024, bn=1024):
    t, k = x.shape
    n = w.shape[1]
    return pl.pallas_call(
        _inproj_kernel,
        grid=(t // bm, n // bn),
        in_specs=[
            pl.BlockSpec((bm, k), lambda i, j: (i, 0)),
            pl.BlockSpec((k, bn), lambda i, j: (0, j)),
            pl.BlockSpec((1, bn), lambda i, j: (0, j)),
        ],
        out_specs=pl.BlockSpec((bm, bn), lambda i, j: (i, j)),
        out_shape=jax.ShapeDtypeStruct((t, n), out_dtype),
        scratch_shapes=[pltpu.VMEM((bm, k), BF16)],
        compiler_params=_params("parallel", "arbitrary"),
        name="inproj",
    )(x, w, colscale)


CONV_ROWS = 64


def _conv_kernel(a_ref, g_ref, ah_ref, gh_ref, w_ref, cb_ref, lg_ref, lb_ref, o_ref, hext_ref, hs_ref, y_ref, wb_ref,
                 *, bt):
    i = pl.program_id(1)
    halo = ah_ref[...] * _sigmoid(gh_ref[...])
    hext_ref[0:CONV_HALO, :] = jnp.where(i > 0, halo, 0.0)
    hext_ref[CONV_HALO:CONV_HALO + bt, :] = a_ref[...] * _sigmoid(g_ref[...])
    span = bt + CONV_HALO - SUBLANES
    for d in range(1, SUBLANES):
        hs_ref[d - 1, 0:span, :] = hext_ref[d:d + span, :]

    first = CONV_HALO - (CONV_WIDTH - 1)
    for j in range(CONV_WIDTH):
        wb_ref[j] = jnp.broadcast_to(w_ref[j:j + 1, :], (SUBLANES, BRANCH))
    wb_ref[CONV_WIDTH] = jnp.broadcast_to(cb_ref[...], (SUBLANES, BRANCH))
    groups = CONV_ROWS // SUBLANES

    def rows(c, carry):
        base = pl.multiple_of(c * CONV_ROWS, CONV_ROWS)
        for cg in range(BRANCH // 256):
            cs = slice(cg * 256, (cg + 1) * 256)
            acc = jnp.broadcast_to(wb_ref[CONV_WIDTH, :, cs][None], (groups, SUBLANES, 256))
            for j in range(CONV_WIDTH):
                off = first + j
                d, al = off % SUBLANES, off - off % SUBLANES
                if d == 0:
                    tap = hext_ref[pl.ds(base + al, CONV_ROWS), cs]
                else:
                    tap = hs_ref[d - 1, pl.ds(base + al, CONV_ROWS), cs]
                acc = acc + wb_ref[j, :, cs][None] * tap.reshape(groups, SUBLANES, 256)
            y_ref[pl.ds(base, CONV_ROWS), cs] = acc.reshape(CONV_ROWS, 256)
        return carry

    lax.fori_loop(0, bt // CONV_ROWS, rows, 0)
    z = _layer_norm_rows(y_ref[...], lg_ref[...], lb_ref[...])
    o_ref[...] = (z * _sigmoid(z)).astype(o_ref.dtype)


def conv_branch(u, conv_w, conv_b, ln_g, ln_b, bt=512):
    b, s, _ = u.shape
    hb = bt // CONV_HALO
    w = jnp.pad(conv_w, ((0, CONV_HALO - CONV_WIDTH), (0, 0)))
    row = lambda v: v.reshape(1, BRANCH)
    vec = pl.BlockSpec((1, BRANCH), lambda bi, i: (0, 0))
    return pl.pallas_call(
        functools.partial(_conv_kernel, bt=bt),
        grid=(b, s // bt),
        in_specs=[
            pl.BlockSpec((None, bt, BRANCH), lambda bi, i: (bi, i, 0)),
            pl.BlockSpec((None, bt, BRANCH), lambda bi, i: (bi, i, 1)),
            pl.BlockSpec((None, CONV_HALO, BRANCH), lambda bi, i: (bi, jnp.maximum(i * hb - 1, 0), 0)),
            pl.BlockSpec((None, CONV_HALO, BRANCH), lambda bi, i: (bi, jnp.maximum(i * hb - 1, 0), 1)),
            pl.BlockSpec((CONV_HALO, BRANCH), lambda bi, i: (0, 0)),
            vec, vec, vec,
        ],
        out_specs=pl.BlockSpec((None, bt, BRANCH), lambda bi, i: (bi, i, 0)),
        out_shape=jax.ShapeDtypeStruct((b, s, BRANCH), BF16),
        scratch_shapes=[
            pltpu.VMEM((bt + CONV_HALO, BRANCH), F32),
            pltpu.VMEM((SUBLANES - 1, bt + CONV_HALO - SUBLANES, BRANCH), F32),
            pltpu.VMEM((bt, BRANCH), F32),
            pltpu.VMEM((CONV_WIDTH + 1, SUBLANES, BRANCH), F32),
        ],
        compiler_params=_params("parallel", "arbitrary"),
        name="conv_branch",
    )(u, u, u, u, w, row(conv_b), row(ln_g), row(ln_b))


HG_HALF = CHUNK // 2
HG_PAIR = 2 * HEAD_DIM
HG_STEPS = 4


def _hgrn_kernel(q_ref, f_ref, v_ref, g_ref, lbl_ref, gn_ref, o_ref, state_ref, *, slot):
    @pl.when(pl.program_id(1) == 0)
    def _():
        state_ref[...] = jnp.zeros_like(state_ref)

    c, hh = CHUNK, HG_HALF
    lbl = lbl_ref[...]
    e = jnp.exp(lbl - jnp.max(lbl, axis=0, keepdims=True))
    upto = lax.broadcasted_iota(jnp.int32, e.shape, 0) <= slot
    lb = jnp.sum(jnp.where(upto, e, 0.0), axis=0, keepdims=True) / jnp.sum(e, axis=0, keepdims=True)
    r_i = lax.broadcasted_iota(jnp.int32, (c, c), 0)
    c_i = lax.broadcasted_iota(jnp.int32, (c, c), 1)
    tri = jnp.where(c_i <= r_i, 1.0, 0.0).astype(BF16)

    def gates(rows):
        f = lb + (1.0 - lb) * _sigmoid(f_ref[rows, :])
        lf = jnp.log(f)
        kk = 1.0 - f
        q = q_ref[rows, :]
        qs = q * _sigmoid(q)
        l3 = jnp.dot(tri, jnp.concatenate(_split3(lf), axis=1), preferred_element_type=F32)
        ll = l3[:, :BRANCH] + l3[:, BRANCH:2 * BRANCH] + l3[:, 2 * BRANCH:]
        l_mid0 = ll[hh // 2 - 1:hh // 2]
        l_edge = ll[hh - 1:hh]
        l_mid1 = ll[hh + hh // 2 - 1:hh + hh // 2]
        l_end = ll[c - 1:c]
        lt, lbt = ll[:hh], ll[hh:]
        return dict(
            a_d0=qs[:hh] * jnp.exp(lt - l_mid0),
            b_d0=kk[:hh] * jnp.exp(l_mid0 - lt),
            a_off=qs[hh:] * jnp.exp(lbt - l_edge),
            b_off=kk[:hh] * jnp.exp(l_edge - lt),
            a_d1=qs[hh:] * jnp.exp(lbt - l_mid1),
            b_d1=kk[hh:] * jnp.exp(l_mid1 - lbt),
            a_int=(qs * jnp.exp(ll)).astype(BF16),
            kd=(kk * jnp.exp(l_end - ll)).astype(BF16),
            dec=jnp.exp(l_end),
        )

    zero = jnp.zeros((hh, HG_PAIR), F32)
    first = lax.broadcasted_iota(jnp.int32, (c, HG_PAIR), 1) < HEAD_DIM
    first3 = jnp.concatenate([first, first, first], axis=1)
    row2 = lax.broadcasted_iota(jnp.int32, (HG_PAIR, HG_PAIR), 0) < HEAD_DIM
    col2 = lax.broadcasted_iota(jnp.int32, (HG_PAIR, HG_PAIR), 1) < HEAD_DIM
    same_head = row2 == col2
    key_i = lax.broadcasted_iota(jnp.int32, (c, 2 * c), 1) % c
    qry_i = lax.broadcasted_iota(jnp.int32, (c, 2 * c), 0)
    stack = lambda top, bottom: jnp.concatenate([top, bottom], axis=0)

    pairs = range(HEADS // 2)
    pss = [slice(p * HG_PAIR, (p + 1) * HG_PAIR) for p in pairs]

    def scores(gt):
        scs = []
        for ps in pss:
            lhs = jnp.concatenate([stack(gt["a_d0"][:, ps], zero), stack(zero, gt["a_off"][:, ps]),
                                   stack(zero, gt["a_d1"][:, ps])], axis=1)
            keys = jnp.concatenate([stack(gt["b_d0"][:, ps], zero), stack(gt["b_off"][:, ps], zero),
                                    stack(zero, gt["b_d1"][:, ps])], axis=1)
            rhs_t = stack(jnp.where(first3, keys, 0.0), jnp.where(first3, 0.0, keys))
            sc = lax.dot_general(lhs.astype(BF16), rhs_t.astype(BF16), NT_DIMS, preferred_element_type=F32)
            scs.append(jnp.where(key_i <= qry_i, sc, 0.0))
        return scs

    chunks = [slice(k * c, (k + 1) * c) for k in range(o_ref.shape[0] // c)]
    gts = [gates(rows) for rows in chunks]
    scss = [scores(gt) for gt in gts]
    states = [state_ref[p] for p in pairs]
    for rows, gt, scs in zip(chunks, gts, scss):
        inter = []
        for p in pairs:
            st = states[p]
            inter.append(lax.dot_general(gt["a_int"][:, pss[p]], st.astype(BF16), NT_DIMS, preferred_element_type=F32))
            upd = lax.dot_general(v_ref[rows, pss[p]].astype(BF16), gt["kd"][:, pss[p]], TN_DIMS,
                                  preferred_element_type=F32)
            states[p] = gt["dec"][:, pss[p]] * st + jnp.where(same_head, upd, 0.0)
        for p in pairs:
            v = v_ref[rows, pss[p]]
            v_bd = stack(jnp.where(first, v, 0.0), jnp.where(first, 0.0, v)).astype(BF16)
            o = jnp.dot(scs[p].astype(BF16), v_bd, preferred_element_type=F32) + inter[p]
            for j in range(2):
                hsl = slice(p * HG_PAIR + j * HEAD_DIM, p * HG_PAIR + (j + 1) * HEAD_DIM)
                oj = o[:, j * HEAD_DIM:(j + 1) * HEAD_DIM]
                oj = oj * lax.rsqrt(jnp.mean(oj * oj, axis=-1, keepdims=True) + LN_EPS)
                g = g_ref[rows, hsl]
                o_ref[rows, hsl] = (oj * gn_ref[:, hsl] * (g * _sigmoid(g))).astype(o_ref.dtype)
    for p in pairs:
        state_ref[p] = states[p]


def hgrn2(u, lb_logits, slot, gnorm_g):
    b, s, _ = u.shape
    slots = lb_logits.shape[0]
    rows = HG_STEPS * CHUNK
    col = lambda j: pl.BlockSpec((None, rows, BRANCH), lambda bi, i: (bi, i, j))
    vec = pl.BlockSpec((1, BRANCH), lambda bi, i: (0, 0))
    return pl.pallas_call(
        functools.partial(_hgrn_kernel, slot=slot),
        grid=(b, s // rows),
        in_specs=[col(2), col(3), col(4), col(5), pl.BlockSpec((slots, BRANCH), lambda bi, i: (0, 0)), vec],
        out_specs=pl.BlockSpec((None, rows, BRANCH), lambda bi, i: (bi, i, 0)),
        out_shape=jax.ShapeDtypeStruct((b, s, BRANCH), BF16),
        scratch_shapes=[pltpu.VMEM((HEADS // 2, HG_PAIR, HG_PAIR), F32)],
        compiler_params=_params("parallel", "arbitrary"),
        name="hgrn2",
    )(u, u, u, u, lb_logits.astype(F32), gnorm_g.reshape(1, BRANCH))


def _fgate_kernel(x_ref, w_ref, b_ref, o_ref, carry_ref, *, bm):
    @pl.when(pl.program_id(1) == 0)
    def _():
        carry_ref[...] = jnp.zeros_like(carry_ref)

    z = jnp.dot(x_ref[...].astype(BF16), w_ref[...], preferred_element_type=F32) + b_ref[...]
    ls = jnp.minimum(z, 0.0) - jnp.log(1.0 + jnp.exp(-jnp.abs(z)))
    r_i = lax.broadcasted_iota(jnp.int32, (bm, bm), 0)
    c_i = lax.broadcasted_iota(jnp.int32, (bm, bm), 1)
    tri = jnp.where(c_i <= r_i, 1.0, 0.0).astype(BF16)
    hi, mid, lo = _split3(ls)
    cum = (jnp.dot(tri, hi, preferred_element_type=F32)
           + jnp.dot(tri, mid, preferred_element_type=F32)
           + jnp.dot(tri, lo, preferred_element_type=F32)) + carry_ref[...]
    carry_ref[...] = cum[bm - 1:bm, :]
    neg = cum * (-LOG2E)
    lane = lax.broadcasted_iota(jnp.int32, (bm, LANES), 1)
    for h in range(HEADS):
        hi, mid, lo = (t.astype(F32) for t in _split3(jnp.broadcast_to(neg[:, h:h + 1], (bm, LANES))))
        o_ref[h] = jnp.where(lane == 0, hi, jnp.where(lane == 1, mid, jnp.where(lane == 2, lo, 0.0))).astype(BF16)


def forget_cumsum(x, w_f, b_f, bm=512):
    b, s, d = x.shape
    w = jnp.pad(w_f, ((0, 0), (0, LANES - HEADS))).astype(BF16)
    bias = jnp.pad(b_f, (0, LANES - HEADS)).reshape(1, LANES).astype(F32)
    return pl.pallas_call(
        functools.partial(_fgate_kernel, bm=bm),
        grid=(b, s // bm),
        in_specs=[
            pl.BlockSpec((None, bm, d), lambda bi, i: (bi, i, 0)),
            pl.BlockSpec((d, LANES), lambda bi, i: (0, 0)),
            pl.BlockSpec((1, LANES), lambda bi, i: (0, 0)),
        ],
        out_specs=pl.BlockSpec((None, HEADS, bm, LANES), lambda bi, i: (bi, 0, i, 0)),
        out_shape=jax.ShapeDtypeStruct((b, HEADS, s, LANES), BF16),
        scratch_shapes=[pltpu.VMEM((1, LANES), F32)],
        compiler_params=_params("parallel", "arbitrary"),
        name="forget_cumsum",
    )(x, w, bias)


FOX_GROUP = 4


def _fox_kernel(q_ref, k_ref, v_ref, f_ref, o_ref, vt_ref, acc_ref, *, bq):
    qi = pl.program_id(2)
    nblk = pl.num_programs(2)
    d = HEAD_DIM
    heads = range(FOX_GROUP)
    hs = [slice(h * d, (h + 1) * d) for h in heads]

    @pl.when(qi == 0)
    def _():
        ones_row = jnp.where(lax.broadcasted_iota(jnp.int32, (ONES_PAD, bq), 0) == 0, 1.0, 0.0).astype(BF16)

        def tr(c, carry):
            cs = pl.multiple_of(c * bq, bq)
            for h in heads:
                vt_ref[h, :d, pl.ds(cs, bq)] = v_ref[pl.ds(cs, bq), hs[h]].astype(F32).T.astype(BF16)
                vt_ref[h, d:, pl.ds(cs, bq)] = ones_row
            return carry
        lax.fori_loop(0, nblk, tr, 0)

    acc_ref[...] = jnp.zeros_like(acc_ref)

    ones3 = jnp.where(lax.broadcasted_iota(jnp.int32, (bq, LANES), 1) < 3, 1.0, 0.0).astype(BF16)
    q_aug = [jnp.concatenate([q_ref[:, hs[h]], ones3], axis=1) for h in heads]

    def step(ks, nk, carry, causal_from):
        sts = []
        for h in heads:
            k_aug = jnp.concatenate([k_ref[pl.ds(ks, nk), hs[h]], f_ref[h, pl.ds(ks, nk), :]], axis=1)
            sts.append(lax.dot_general(k_aug, q_aug[h], NT_DIMS, preferred_element_type=F32))
        if causal_from is not None:
            keep = (lax.broadcasted_iota(jnp.int32, (nk, bq), 0) - causal_from
                    <= lax.broadcasted_iota(jnp.int32, (nk, bq), 1))
            sts = [jnp.where(keep, st, NEG) for st in sts]
        out = []
        for h in heads:
            m = carry[h]
            m_new = jnp.maximum(m, jnp.max(sts[h], axis=0, keepdims=True))
            p = jnp.exp2(sts[h] - m_new).astype(BF16)
            pv = jnp.dot(vt_ref[h, :, pl.ds(ks, nk)], p, preferred_element_type=F32)
            acc_ref[h] = jnp.exp2(m - m_new) * acc_ref[h] + pv
            out.append(m_new)
        return tuple(out)

    def finish(ks, nk, carry, causal_from):
        step(ks, nk, carry, causal_from)
        for h in heads:
            o_ref[:, hs[h]] = (acc_ref[h, :d, :] / acc_ref[h, d:d + 1, :]).T.astype(o_ref.dtype)

    init = tuple(jnp.full((1, bq), NEG, F32) for _ in heads)
    carry = lax.fori_loop(0, qi // 2, lambda t, cr: step(pl.multiple_of(t * 2 * bq, 2 * bq), 2 * bq, cr, None), init)

    @pl.when(qi % 2 == 0)
    def _():
        finish(pl.multiple_of(qi * bq, bq), bq, carry, 0)

    @pl.when(qi % 2 == 1)
    def _():
        finish(pl.multiple_of((qi - 1) * bq, 2 * bq), 2 * bq, carry, bq)


def fox_attention(qkv, fcum, q_col, k_col, v_col, bq=512):
    b, s, _ = qkv.shape
    g = FOX_GROUP
    w = g * HEAD_DIM
    assert q_col % g == 0 and k_col % g == 0 and v_col % g == 0
    return pl.pallas_call(
        functools.partial(_fox_kernel, bq=bq),
        grid=(b, HEADS // g, s // bq),
        in_specs=[
            pl.BlockSpec((None, bq, w), lambda bi, h, i: (bi, i, q_col // g + h)),
            pl.BlockSpec((None, s, w), lambda bi, h, i: (bi, 0, k_col // g + h), pipeline_mode=pl.Buffered(1)),
            pl.BlockSpec((None, s, w), lambda bi, h, i: (bi, 0, v_col // g + h), pipeline_mode=pl.Buffered(1)),
            pl.BlockSpec((None, g, s, LANES), lambda bi, h, i: (bi, h, 0, 0), pipeline_mode=pl.Buffered(1)),
        ],
        out_specs=pl.BlockSpec((None, bq, w), lambda bi, h, i: (bi, i, h)),
        out_shape=jax.ShapeDtypeStruct((b, s, BRANCH), BF16),
        scratch_shapes=[pltpu.VMEM((g, HEAD_DIM + ONES_PAD, s), BF16), pltpu.VMEM((g, HEAD_DIM + ONES_PAD, bq), F32)],
        compiler_params=_params("parallel", "parallel", "arbitrary"),
        name="fox_attention",
    )(qkv, qkv, qkv, fcum)


CA_GROUP = 4


def _chunk_attn_kernel(q_ref, kp_ref, kc_ref, vp_ref, vc_ref, row_ref, o_ref, bias_ref, *, bq):
    qi = pl.program_id(2)
    d = HEAD_DIM
    hq = bq // 2
    heads = range(CA_GROUP)
    hs = [slice(h * d, (h + 1) * d) for h in heads]

    @pl.when((pl.program_id(1) == 0) & (qi == 0))
    def _():
        w = 2 * bq
        qc = lax.broadcasted_iota(jnp.int32, (hq, w), 0) // CHUNK
        kc = lax.broadcasted_iota(jnp.int32, (hq, w), 1) // CHUNK - bq // CHUNK
        for h in heads:
            toeplitz = pltpu.roll(jnp.broadcast_to(row_ref[h], (hq, w)), 0, 1, stride=1, stride_axis=0)
            band = jnp.where(kc <= qc, jnp.where(kc >= qc - CA_LEFT_CHUNKS, toeplitz, NEG), NEG)
            bias_ref[h] = band[:, :3 * hq].T

    row_i = lax.broadcasted_iota(jnp.int32, (3 * hq, hq), 0)
    keep = [row_i >= jnp.where(qi == 0, bq, 0), row_i >= jnp.where(qi == 0, hq, 0)]
    ones_rows = jnp.where(lax.broadcasted_iota(jnp.int32, (ONES_PAD, bq), 0) == 0, 1.0, 0.0).astype(BF16)

    sts = []
    for h in heads:
        kp, kc_ = kp_ref[:, hs[h]], kc_ref[:, hs[h]]
        windows = [jnp.concatenate([kp, kc_[:hq]], axis=0), jnp.concatenate([kp[hq:], kc_], axis=0)]
        for j in range(2):
            s = lax.dot_general(windows[j], q_ref[j * hq:(j + 1) * hq, hs[h]], NT_DIMS,
                                preferred_element_type=F32) + bias_ref[h]
            sts.append(jnp.where(keep[j], s, NEG))
    for h in heads:
        vtp = jnp.concatenate([vp_ref[:, hs[h]].astype(F32).T.astype(BF16), ones_rows], axis=0)
        vtc = jnp.concatenate([vc_ref[:, hs[h]].astype(F32).T.astype(BF16), ones_rows], axis=0)
        windows = [jnp.concatenate([vtp, vtc[:, :hq]], axis=1), jnp.concatenate([vtp[:, hq:], vtc], axis=1)]
        for j in range(2):
            s = sts[2 * h + j]
            p = jnp.exp(s - jnp.max(s, axis=0, keepdims=True)).astype(BF16)
            o = jnp.dot(windows[j], p, preferred_element_type=F32)
            o_ref[j * hq:(j + 1) * hq, hs[h]] = (o[:d] / o[d:d + 1]).T.astype(o_ref.dtype)


def _distance_rows(rel_bias, bq):
    heads, table = rel_bias.shape
    assert table == (CHUNK - 1) + REL_CLIP + 1 and bq >= REL_CLIP
    far = rel_bias[:, table - 1:]
    return jnp.concatenate([
        jnp.broadcast_to(far, (heads, bq - REL_CLIP)),
        rel_bias[:, ::-1],
        jnp.broadcast_to(far, (heads, bq - CHUNK)),
    ], axis=1).astype(F32).reshape(heads, 1, 2 * bq)


def chunk_attention(qkv, rel_bias, q_col, k_col, v_col, bq=512):
    b, s, _ = qkv.shape
    assert bq == CA_LEFT_CHUNKS * CHUNK
    rows = _distance_rows(rel_bias, bq)
    g = CA_GROUP
    w = g * HEAD_DIM
    assert q_col % g == 0 and k_col % g == 0 and v_col % g == 0
    cur = lambda col: pl.BlockSpec((None, bq, w), lambda h, bi, i: (bi, i, col // g + h))
    prev = lambda col: pl.BlockSpec((None, bq, w), lambda h, bi, i: (bi, jnp.maximum(i - 1, 0), col // g + h))
    return pl.pallas_call(
        functools.partial(_chunk_attn_kernel, bq=bq),
        grid=(HEADS // g, b, s // bq),
        in_specs=[cur(q_col), prev(k_col), cur(k_col), prev(v_col), cur(v_col),
                  pl.BlockSpec((g, 1, 2 * bq), lambda h, bi, i: (h, 0, 0))],
        out_specs=pl.BlockSpec((None, bq, w), lambda h, bi, i: (bi, i, h)),
        out_shape=jax.ShapeDtypeStruct((b, s, BRANCH), BF16),
        scratch_shapes=[pltpu.VMEM((g, 3 * bq // 2, bq // 2), F32)],
        compiler_params=_params("arbitrary", "arbitrary", "arbitrary"),
        name="chunk_attention",
    )(qkv, qkv, qkv, qkv, qkv, rows)


def _outproj_ln_kernel(a_ref, b_ref, wa_ref, wb_ref, x_ref, g_ref, beta_ref, o_ref):
    mix = (jnp.dot(a_ref[...], wa_ref[...], preferred_element_type=F32)
           + jnp.dot(b_ref[...], wb_ref[...], preferred_element_type=F32))
    o_ref[...] = _layer_norm_rows(ALPHA * x_ref[...] + mix, g_ref[...], beta_ref[...])


def outproj_ln(a, b, w_out, x, g, beta, bm=512):
    t, d = x.shape
    vec = pl.BlockSpec((1, d), lambda i: (0, 0))
    act = pl.BlockSpec((bm, BRANCH), lambda i: (i, 0))
    return pl.pallas_call(
        _outproj_ln_kernel,
        grid=(t // bm,),
        in_specs=[act, act,
                  pl.BlockSpec((BRANCH, d), lambda i: (0, 0)), pl.BlockSpec((BRANCH, d), lambda i: (1, 0)),
                  pl.BlockSpec((bm, d), lambda i: (i, 0)), vec, vec],
        out_specs=pl.BlockSpec((bm, d), lambda i: (i, 0)),
        out_shape=jax.ShapeDtypeStruct((t, d), F32),
        compiler_params=_params("parallel"),
        name="outproj_ln",
    )(a, b, w_out, w_out, x, g.reshape(1, d), beta.reshape(1, d))


def _mlp_ln_kernel(x_ref, w1_ref, w2_ref, g_ref, beta_ref, o_ref, xb_ref, acc_ref):
    j = pl.program_id(1)

    @pl.when(j == 0)
    def _():
        xb_ref[...] = x_ref[...].astype(BF16)
        acc_ref[...] = jnp.zeros_like(acc_ref)

    h = jnp.maximum(jnp.dot(xb_ref[...], w1_ref[...], preferred_element_type=F32), 0.0)
    acc_ref[...] += jnp.dot((h * h).astype(BF16), w2_ref[...], preferred_element_type=F32)

    @pl.when(j == pl.num_programs(1) - 1)
    def _():
        o_ref[...] = _layer_norm_rows(ALPHA * x_ref[...] + acc_ref[...], g_ref[...], beta_ref[...])


def mlp_ln(x, w1, w2, g, beta, bm=512, bf=1024):
    t, d = x.shape
    ff = w1.shape[1]
    vec = pl.BlockSpec((1, d), lambda i, j: (0, 0))
    return pl.pallas_call(
        _mlp_ln_kernel,
        grid=(t // bm, ff // bf),
        in_specs=[
            pl.BlockSpec((bm, d), lambda i, j: (i, 0)),
            pl.BlockSpec((d, bf), lambda i, j: (0, j)),
            pl.BlockSpec((bf, d), lambda i, j: (j, 0)),
            vec, vec,
        ],
        out_specs=pl.BlockSpec((bm, d), lambda i, j: (i, 0)),
        out_shape=jax.ShapeDtypeStruct((t, d), F32),
        scratch_shapes=[pltpu.VMEM((bm, d), BF16), pltpu.VMEM((bm, d), F32)],
        compiler_params=_params("parallel", "arbitrary"),
        name="mlp_ln",
    )(x, w1, w2, g.reshape(1, d), beta.reshape(1, d))


def even_mixer(x, w_in, conv_w, conv_b, conv_ln_g, conv_ln_b, lb_logits, slot, gnorm_g):
    b, s, d = x.shape
    n = w_in.shape[1]
    u = inproj(x.reshape(b * s, d), w_in, jnp.ones((1, n), F32), F32).reshape(b, s, n)
    a_out = conv_branch(u, conv_w, conv_b, conv_ln_g, conv_ln_b)
    b_out = hgrn2(u, lb_logits, slot, gnorm_g)
    return a_out.reshape(b * s, BRANCH), b_out.reshape(b * s, BRANCH)


def odd_mixer(x, w_in, b_f, rel_bias):
    b, s, d = x.shape
    w_qkv = jnp.concatenate([w_in[:, :3 * BRANCH], w_in[:, 3 * BRANCH + HEADS:]], axis=1).astype(BF16)
    w_f = w_in[:, 3 * BRANCH:3 * BRANCH + HEADS]
    scale = HEAD_DIM ** -0.5
    ones = jnp.ones((BRANCH,), F32)
    colscale = jnp.concatenate([ones * (scale * LOG2E), ones, ones, ones * scale, ones, ones]).reshape(1, 6 * BRANCH)
    qkv = inproj(x.reshape(b * s, d), w_qkv, colscale, BF16).reshape(b, s, 6 * BRANCH)
    fcum = forget_cumsum(x, w_f, b_f)
    c_out = fox_attention(qkv, fcum, 0, HEADS, 2 * HEADS)
    d_out = chunk_attention(qkv, rel_bias, 3 * HEADS, 4 * HEADS, 5 * HEADS)
    return c_out.reshape(b * s, BRANCH), d_out.reshape(b * s, BRANCH)


def kernel(x, ev_w_in, ev_conv_w, ev_conv_b, ev_conv_ln_g, ev_conv_ln_b, hgrn_lb_logits, ev_gnorm_g, ev_w_out, od_w_in, fox_b_f, rel_bias, od_w_out, ln_mix_g, ln_mix_b, mlp_w1, mlp_w2, ln_mlp_g, ln_mlp_b):
    b, s, d = x.shape
    for l in range(DEPTH):
        j = l // 2
        if l % 2 == 0:
            p, r = even_mixer(x, cast_layer_bf16(ev_w_in, j), ev_conv_w[j], ev_conv_b[j], ev_conv_ln_g[j],
                              ev_conv_ln_b[j], hgrn_lb_logits, j, ev_gnorm_g[j])
            w_out = cast_layer_bf16(ev_w_out, j)
        else:
            p, r = odd_mixer(x, od_w_in[j], fox_b_f[j], rel_bias[j])
            w_out = cast_layer_bf16(od_w_out, j)
        x2 = outproj_ln(p, r, w_out, x.reshape(b * s, d), ln_mix_g[l], ln_mix_b[l])
        x2 = mlp_ln(x2, cast_layer_bf16(mlp_w1, l), cast_layer_bf16(mlp_w2, l), ln_mlp_g[l], ln_mlp_b[l])
        x = x2.reshape(b, s, d)
    return x
```

```python
import functools

import jax
import jax.numpy as jnp
from jax import lax
from jax.experimental import pallas as pl
from jax.experimental.pallas import tpu as pltpu

F32 = jnp.float32
BF16 = jnp.bfloat16

D_MODEL = 2048
DEPTH = 2
CHUNK = 64
LN_EPS = 1e-5
ALPHA = (2 * DEPTH) ** 0.25
HEADS = 8
HEAD_DIM = 128
BRANCH = HEADS * HEAD_DIM
CONV_WIDTH = 31
CONV_HALO = 32
CA_LEFT_CHUNKS = 8
REL_CLIP = 256
D_FF = 4 * D_MODEL
NEG = -1e30
LOG2E = 1.4426950408889634

ONES_PAD = 16
LANES = 128
SUBLANES = 8
VMEM_LIMIT = 56 * 1024 * 1024

NT_DIMS = (((1,), (1,)), ((), ()))
TN_DIMS = (((0,), (0,)), ((), ()))


def _params(*sem):
    return pltpu.CompilerParams(dimension_semantics=sem, vmem_limit_bytes=VMEM_LIMIT)


def _sigmoid(x):
    return 1.0 / (1.0 + jnp.exp(-x))


def _layer_norm_rows(y, g, b):
    mu = jnp.mean(y, axis=-1, keepdims=True)
    d = y - mu
    var = jnp.mean(d * d, axis=-1, keepdims=True)
    return d * lax.rsqrt(var + LN_EPS) * g + b


def _split3(x):
    hi = x.astype(BF16)
    r = x - hi.astype(F32)
    mid = r.astype(BF16)
    lo = (r - mid.astype(F32)).astype(BF16)
    return hi, mid, lo


def _cast_kernel(w_ref, o_ref):
    o_ref[...] = w_ref[...].astype(o_ref.dtype)


def cast_layer_bf16(w, layer, br=512, bc=2048):
    _, r, c = w.shape
    br, bc = min(br, r), min(bc, c)
    return pl.pallas_call(
        _cast_kernel,
        grid=(r // br, c // bc),
        in_specs=[pl.BlockSpec((None, br, bc), lambda i, j: (layer, i, j))],
        out_specs=pl.BlockSpec((br, bc), lambda i, j: (i, j)),
        out_shape=jax.ShapeDtypeStruct((r, c), BF16),
        compiler_params=_params("parallel", "parallel"),
        name="cast_bf16",
    )(w)


def _inproj_kernel(x_ref, w_ref, s_ref, o_ref, xb_ref):
    @pl.when(pl.program_id(1) == 0)
    def _():
        xb_ref[...] = x_ref[...].astype(BF16)

    acc = jnp.dot(xb_ref[...], w_ref[...], preferred_element_type=F32)
    o_ref[...] = (acc * s_ref[...]).astype(o_ref.dtype)


def inproj(x, w, colscale, out_dtype, bm=1024, bn=1024):
    t, k = x.shape
    n = w.shape[1]
    return pl.pallas_call(
        _inproj_kernel,
        grid=(t // bm, n // bn),
        in_specs=[
            pl.BlockSpec((bm, k), lambda i, j: (i, 0)),
            pl.BlockSpec((k, bn), lambda i, j: (0, j)),
            pl.BlockSpec((1, bn), lambda i, j: (0, j)),
        ],
        out_specs=pl.BlockSpec((bm, bn), lambda i, j: (i, j)),
        out_shape=jax.ShapeDtypeStruct((t, n), out_dtype),
        scratch_shapes=[pltpu.VMEM((bm, k), BF16)],
        compiler_params=_params("parallel", "arbitrary"),
        name="inproj",
    )(x, w, colscale)


CONV_ROWS = 64


def _conv_kernel(a_ref, g_ref, ah_ref, gh_ref, w_ref, cb_ref, lg_ref, lb_ref, o_ref, hext_ref, hs_ref, y_ref, wb_ref,
                 *, bt):
    i = pl.program_id(1)
    halo = ah_ref[...] * _sigmoid(gh_ref[...])
    hext_ref[0:CONV_HALO, :] = jnp.where(i > 0, halo, 0.0)
    hext_ref[CONV_HALO:CONV_HALO + bt, :] = a_ref[...] * _sigmoid(g_ref[...])
    span = bt + CONV_HALO - SUBLANES
    for d in range(1, SUBLANES):
        hs_ref[d - 1, 0:span, :] = hext_ref[d:d + span, :]

    first = CONV_HALO - (CONV_WIDTH - 1)
    for j in range(CONV_WIDTH):
        wb_ref[j] = jnp.broadcast_to(w_ref[j:j + 1, :], (SUBLANES, BRANCH))
    wb_ref[CONV_WIDTH] = jnp.broadcast_to(cb_ref[...], (SUBLANES, BRANCH))
    groups = CONV_ROWS // SUBLANES

    def rows(c, carry):
        base = pl.multiple_of(c * CONV_ROWS, CONV_ROWS)
        for cg in range(BRANCH // 256):
            cs = slice(cg * 256, (cg + 1) * 256)
            acc = jnp.broadcast_to(wb_ref[CONV_WIDTH, :, cs][None], (groups, SUBLANES, 256))
            for j in range(CONV_WIDTH):
                off = first + j
                d, al = off % SUBLANES, off - off % SUBLANES
                if d == 0:
                    tap = hext_ref[pl.ds(base + al, CONV_ROWS), cs]
                else:
                    tap = hs_ref[d - 1, pl.ds(base + al, CONV_ROWS), cs]
                acc = acc + wb_ref[j, :, cs][None] * tap.reshape(groups, SUBLANES, 256)
            y_ref[pl.ds(base, CONV_ROWS), cs] = acc.reshape(CONV_ROWS, 256)
        return carry

    lax.fori_loop(0, bt // CONV_ROWS, rows, 0)
    z = _layer_norm_rows(y_ref[...], lg_ref[...], lb_ref[...])
    o_ref[...] = (z * _sigmoid(z)).astype(o_ref.dtype)


def conv_branch(u, conv_w, conv_b, ln_g, ln_b, bt=512):
    b, s, _ = u.shape
    hb = bt // CONV_HALO
    w = jnp.pad(conv_w, ((0, CONV_HALO - CONV_WIDTH), (0, 0)))
    row = lambda v: v.reshape(1, BRANCH)
    vec = pl.BlockSpec((1, BRANCH), lambda bi, i: (0, 0))
    return pl.pallas_call(
        functools.partial(_conv_kernel, bt=bt),
        grid=(b, s // bt),
        in_specs=[
            pl.BlockSpec((None, bt, BRANCH), lambda bi, i: (bi, i, 0)),
            pl.BlockSpec((None, bt, BRANCH), lambda bi, i: (bi, i, 1)),
            pl.BlockSpec((None, CONV_HALO, BRANCH), lambda bi, i: (bi, jnp.maximum(i * hb - 1, 0), 0)),
            pl.BlockSpec((None, CONV_HALO, BRANCH), lambda bi, i: (bi, jnp.maximum(i * hb - 1, 0), 1)),
            pl.BlockSpec((CONV_HALO, BRANCH), lambda bi, i: (0, 0)),
            vec, vec, vec,
        ],
        out_specs=pl.BlockSpec((None, bt, BRANCH), lambda bi, i: (bi, i, 0)),
        out_shape=jax.ShapeDtypeStruct((b, s, BRANCH), BF16),
        scratch_shapes=[
            pltpu.VMEM((bt + CONV_HALO, BRANCH), F32),
            pltpu.VMEM((SUBLANES - 1, bt + CONV_HALO - SUBLANES, BRANCH), F32),
            pltpu.VMEM((bt, BRANCH), F32),
            pltpu.VMEM((CONV_WIDTH + 1, SUBLANES, BRANCH), F32),
        ],
        compiler_params=_params("parallel", "arbitrary"),
        name="conv_branch",
    )(u, u, u, u, w, row(conv_b), row(ln_g), row(ln_b))


HG_HALF = CHUNK // 2
HG_PAIR = 2 * HEAD_DIM
HG_STEPS = 4


def _hgrn_kernel(q_ref, f_ref, v_ref, g_ref, lbl_ref, gn_ref, o_ref, state_ref, *, slot):
    @pl.when(pl.program_id(1) == 0)
    def _():
        state_ref[...] = jnp.zeros_like(state_ref)

    c, hh = CHUNK, HG_HALF
    lbl = lbl_ref[...]
    e = jnp.exp(lbl - jnp.max(lbl, axis=0, keepdims=True))
    upto = lax.broadcasted_iota(jnp.int32, e.shape, 0) <= slot
    lb = jnp.sum(jnp.where(upto, e, 0.0), axis=0, keepdims=True) / jnp.sum(e, axis=0, keepdims=True)
    r_i = lax.broadcasted_iota(jnp.int32, (c, c), 0)
    c_i = lax.broadcasted_iota(jnp.int32, (c, c), 1)
    tri = jnp.where(c_i <= r_i, 1.0, 0.0).astype(BF16)

    def gates(rows):
        f = lb + (1.0 - lb) * _sigmoid(f_ref[rows, :])
        lf = jnp.log(f)
        kk = 1.0 - f
        q = q_ref[rows, :]
        qs = q * _sigmoid(q)
        l3 = jnp.dot(tri, jnp.concatenate(_split3(lf), axis=1), preferred_element_type=F32)
        ll = l3[:, :BRANCH] + l3[:, BRANCH:2 * BRANCH] + l3[:, 2 * BRANCH:]
        l_mid0 = ll[hh // 2 - 1:hh // 2]
        l_edge = ll[hh - 1:hh]
        l_mid1 = ll[hh + hh // 2 - 1:hh + hh // 2]
        l_end = ll[c - 1:c]
        lt, lbt = ll[:hh], ll[hh:]
        return dict(
            a_d0=qs[:hh] * jnp.exp(lt - l_mid0),
            b_d0=kk[:hh] * jnp.exp(l_mid0 - lt),
            a_off=qs[hh:] * jnp.exp(lbt - l_edge),
            b_off=kk[:hh] * jnp.exp(l_edge - lt),
            a_d1=qs[hh:] * jnp.exp(lbt - l_mid1),
            b_d1=kk[hh:] * jnp.exp(l_mid1 - lbt),
            a_int=(qs * jnp.exp(ll)).astype(BF16),
            kd=(kk * jnp.exp(l_end - ll)).astype(BF16),
            dec=jnp.exp(l_end),
        )

    zero = jnp.zeros((hh, HG_PAIR), F32)
    first = lax.broadcasted_iota(jnp.int32, (c, HG_PAIR), 1) < HEAD_DIM
    first3 = jnp.concatenate([first, first, first], axis=1)
    row2 = lax.broadcasted_iota(jnp.int32, (HG_PAIR, HG_PAIR), 0) < HEAD_DIM
    col2 = lax.broadcasted_iota(jnp.int32, (HG_PAIR, HG_PAIR), 1) < HEAD_DIM
    same_head = row2 == col2
    key_i = lax.broadcasted_iota(jnp.int32, (c, 2 * c), 1) % c
    qry_i = lax.broadcasted_iota(jnp.int32, (c, 2 * c), 0)
    stack = lambda top, bottom: jnp.concatenate([top, bottom], axis=0)

    pairs = range(HEADS // 2)
    pss = [slice(p * HG_PAIR, (p + 1) * HG_PAIR) for p in pairs]

    def scores(gt):
        scs = []
        for ps in pss:
            lhs = jnp.concatenate([stack(gt["a_d0"][:, ps], zero), stack(zero, gt["a_off"][:, ps]),
                                   stack(zero, gt["a_d1"][:, ps])], axis=1)
            keys = jnp.concatenate([stack(gt["b_d0"][:, ps], zero), stack(gt["b_off"][:, ps], zero),
                                    stack(zero, gt["b_d1"][:, ps])], axis=1)
            rhs_t = stack(jnp.where(first3, keys, 0.0), jnp.where(first3, 0.0, keys))
            sc = lax.dot_general(lhs.astype(BF16), rhs_t.astype(BF16), NT_DIMS, preferred_element_type=F32)
            scs.append(jnp.where(key_i <= qry_i, sc, 0.0))
        return scs

    chunks = [slice(k * c, (k + 1) * c) for k in range(o_ref.shape[0] // c)]
    gts = [gates(rows) for rows in chunks]
    scss = [scores(gt) for gt in gts]
    states = [state_ref[p] for p in pairs]
    for rows, gt, scs in zip(chunks, gts, scss):
        inter = []
        for p in pairs:
            st = states[p]
            inter.append(lax.dot_general(gt["a_int"][:, pss[p]], st.astype(BF16), NT_DIMS, preferred_element_type=F32))
            upd = lax.dot_general(v_ref[rows, pss[p]].astype(BF16), gt["kd"][:, pss[p]], TN_DIMS,
                                  preferred_element_type=F32)
            states[p] = gt["dec"][:, pss[p]] * st + jnp.where(same_head, upd, 0.0)
        for p in pairs:
            v = v_ref[rows, pss[p]]
            v_bd = stack(jnp.where(first, v, 0.0), jnp.where(first, 0.0, v)).astype(BF16)
            o = jnp.dot(scs[p].astype(BF16), v_bd, preferred_element_type=F32) + inter[p]
            for j in range(2):
                hsl = slice(p * HG_PAIR + j * HEAD_DIM, p * HG_PAIR + (j + 1) * HEAD_DIM)
                oj = o[:, j * HEAD_DIM:(j + 1) * HEAD_DIM]
                oj = oj * lax.rsqrt(jnp.mean(oj * oj, axis=-1, keepdims=True) + LN_EPS)
                g = g_ref[rows, hsl]
                o_ref[rows, hsl] = (oj * gn_ref[:, hsl] * (g * _sigmoid(g))).astype(o_ref.dtype)
    for p in pairs:
        state_ref[p] = states[p]


def hgrn2(u, lb_logits, slot, gnorm_g):
    b, s, _ = u.shape
    slots = lb_logits.shape[0]
    rows = HG_STEPS * CHUNK
    col = lambda j: pl.BlockSpec((None, rows, BRANCH), lambda bi, i: (bi, i, j))
    vec = pl.BlockSpec((1, BRANCH), lambda bi, i: (0, 0))
    return pl.pallas_call(
        functools.partial(_hgrn_kernel, slot=slot),
        grid=(b, s // rows),
        in_specs=[col(2), col(3), col(4), col(5), pl.BlockSpec((slots, BRANCH), lambda bi, i: (0, 0)), vec],
        out_specs=pl.BlockSpec((None, rows, BRANCH), lambda bi, i: (bi, i, 0)),
        out_shape=jax.ShapeDtypeStruct((b, s, BRANCH), BF16),
        scratch_shapes=[pltpu.VMEM((HEADS // 2, HG_PAIR, HG_PAIR), F32)],
        compiler_params=_params("parallel", "arbitrary"),
        name="hgrn2",
    )(u, u, u, u, lb_logits.astype(F32), gnorm_g.reshape(1, BRANCH))


def _fgate_kernel(x_ref, w_ref, b_ref, o_ref, carry_ref, *, bm):
    @pl.when(pl.program_id(1) == 0)
    def _():
        carry_ref[...] = jnp.zeros_like(carry_ref)

    z = jnp.dot(x_ref[...].astype(BF16), w_ref[...], preferred_element_type=F32) + b_ref[...]
    ls = jnp.minimum(z, 0.0) - jnp.log(1.0 + jnp.exp(-jnp.abs(z)))
    r_i = lax.broadcasted_iota(jnp.int32, (bm, bm), 0)
    c_i = lax.broadcasted_iota(jnp.int32, (bm, bm), 1)
    tri = jnp.where(c_i <= r_i, 1.0, 0.0).astype(BF16)
    hi, mid, lo = _split3(ls)
    cum = (jnp.dot(tri, hi, preferred_element_type=F32)
           + jnp.dot(tri, mid, preferred_element_type=F32)
           + jnp.dot(tri, lo, preferred_element_type=F32)) + carry_ref[...]
    carry_ref[...] = cum[bm - 1:bm, :]
    neg = cum * (-LOG2E)
    lane = lax.broadcasted_iota(jnp.int32, (bm, LANES), 1)
    for h in range(HEADS):
        hi, mid, lo = (t.astype(F32) for t in _split3(jnp.broadcast_to(neg[:, h:h + 1], (bm, LANES))))
        o_ref[h] = jnp.where(lane == 0, hi, jnp.where(lane == 1, mid, jnp.where(lane == 2, lo, 0.0))).astype(BF16)


def forget_cumsum(x, w_f, b_f, bm=512):
    b, s, d = x.shape
    w = jnp.pad(w_f, ((0, 0), (0, LANES - HEADS))).astype(BF16)
    bias = jnp.pad(b_f, (0, LANES - HEADS)).reshape(1, LANES).astype(F32)
    return pl.pallas_call(
        functools.partial(_fgate_kernel, bm=bm),
        grid=(b, s // bm),
        in_specs=[
            pl.BlockSpec((None, bm, d), lambda bi, i: (bi, i, 0)),
            pl.BlockSpec((d, LANES), lambda bi, i: (0, 0)),
            pl.BlockSpec((1, LANES), lambda bi, i: (0, 0)),
        ],
        out_specs=pl.BlockSpec((None, HEADS, bm, LANES), lambda bi, i: (bi, 0, i, 0)),
        out_shape=jax.ShapeDtypeStruct((b, HEADS, s, LANES), BF16),
        scratch_shapes=[pltpu.VMEM((1, LANES), F32)],
        compiler_params=_params("parallel", "arbitrary"),
        name="forget_cumsum",
    )(x, w, bias)


FOX_GROUP = 4


def _fox_kernel(q_ref, k_ref, v_ref, f_ref, o_ref, vt_ref, acc_ref, *, bq):
    qi = pl.program_id(2)
    nblk = pl.num_programs(2)
    d = HEAD_DIM
    heads = range(FOX_GROUP)
    hs = [slice(h * d, (h + 1) * d) for h in heads]

    @pl.when(qi == 0)
    def _():
        ones_row = jnp.where(lax.broadcasted_iota(jnp.int32, (ONES_PAD, bq), 0) == 0, 1.0, 0.0).astype(BF16)

        def tr(c, carry):
            cs = pl.multiple_of(c * bq, bq)
            for h in heads:
                vt_ref[h, :d, pl.ds(cs, bq)] = v_ref[pl.ds(cs, bq), hs[h]].T
                vt_ref[h, d:, pl.ds(cs, bq)] = ones_row
            return carry
        lax.fori_loop(0, nblk, tr, 0)

    acc_ref[...] = jnp.zeros_like(acc_ref)

    ones3 = jnp.where(lax.broadcasted_iota(jnp.int32, (bq, LANES), 1) < 3, 1.0, 0.0).astype(BF16)
    q_aug = [jnp.concatenate([q_ref[:, hs[h]], ones3], axis=1) for h in heads]

    def step(ks, nk, carry, causal_from):
        sts = []
        for h in heads:
            k_aug = jnp.concatenate([k_ref[pl.ds(ks, nk), hs[h]], f_ref[h, pl.ds(ks, nk), :]], axis=1)
            sts.append(lax.dot_general(k_aug, q_aug[h], NT_DIMS, preferred_element_type=F32))
        if causal_from is not None:
            keep = (lax.broadcasted_iota(jnp.int32, (nk, bq), 0) - causal_from
                    <= lax.broadcasted_iota(jnp.int32, (nk, bq), 1))
            sts = [jnp.where(keep, st, NEG) for st in sts]
        out = []
        for h in heads:
            m = carry[h]
            m_new = jnp.maximum(m, jnp.max(sts[h], axis=0, keepdims=True))
            p = jnp.exp2(sts[h] - m_new).astype(BF16)
            pv = jnp.dot(vt_ref[h, :, pl.ds(ks, nk)], p, preferred_element_type=F32)
            acc_ref[h] = jnp.exp2(m - m_new) * acc_ref[h] + pv
            out.append(m_new)
        return tuple(out)

    def finish(ks, nk, carry, causal_from):
        step(ks, nk, carry, causal_from)
        for h in heads:
            o_ref[:, hs[h]] = (acc_ref[h, :d, :] / acc_ref[h, d:d + 1, :]).astype(o_ref.dtype).T

    init = tuple(jnp.full((1, bq), NEG, F32) for _ in heads)
    carry = lax.fori_loop(0, qi // 2, lambda t, cr: step(pl.multiple_of(t * 2 * bq, 2 * bq), 2 * bq, cr, None), init)

    @pl.when(qi % 2 == 0)
    def _():
        finish(pl.multiple_of(qi * bq, bq), bq, carry, 0)

    @pl.when(qi % 2 == 1)
    def _():
        finish(pl.multiple_of((qi - 1) * bq, 2 * bq), 2 * bq, carry, bq)


def fox_attention(qkv, fcum, q_col, k_col, v_col, bq=512):
    b, s, _ = qkv.shape
    g = FOX_GROUP
    w = g * HEAD_DIM
    assert q_col % g == 0 and k_col % g == 0 and v_col % g == 0
    return pl.pallas_call(
        functools.partial(_fox_kernel, bq=bq),
        grid=(b, HEADS // g, s // bq),
        in_specs=[
            pl.BlockSpec((None, bq, w), lambda bi, h, i: (bi, i, q_col // g + h)),
            pl.BlockSpec((None, s, w), lambda bi, h, i: (bi, 0, k_col // g + h), pipeline_mode=pl.Buffered(1)),
            pl.BlockSpec((None, s, w), lambda bi, h, i: (bi, 0, v_col // g + h), pipeline_mode=pl.Buffered(1)),
            pl.BlockSpec((None, g, s, LANES), lambda bi, h, i: (bi, h, 0, 0), pipeline_mode=pl.Buffered(1)),
        ],
        out_specs=pl.BlockSpec((None, bq, w), lambda bi, h, i: (bi, i, h)),
        out_shape=jax.ShapeDtypeStruct((b, s, BRANCH), BF16),
        scratch_shapes=[pltpu.VMEM((g, HEAD_DIM + ONES_PAD, s), BF16), pltpu.VMEM((g, HEAD_DIM + ONES_PAD, bq), F32)],
        compiler_params=_params("parallel", "parallel", "arbitrary"),
        name="fox_attention",
    )(qkv, qkv, qkv, fcum)


CA_GROUP = 4


def _chunk_attn_kernel(q_ref, kp_ref, kc_ref, vp_ref, vc_ref, row_ref, o_ref, bias_ref, *, bq):
    qi = pl.program_id(2)
    d = HEAD_DIM
    hq = bq // 2
    heads = range(CA_GROUP)
    hs = [slice(h * d, (h + 1) * d) for h in heads]

    @pl.when((pl.program_id(1) == 0) & (qi == 0))
    def _():
        w = 2 * bq
        qc = lax.broadcasted_iota(jnp.int32, (hq, w), 0) // CHUNK
        kc = lax.broadcasted_iota(jnp.int32, (hq, w), 1) // CHUNK - bq // CHUNK
        key_row = lax.broadcasted_iota(jnp.int32, (3 * hq, hq), 0)
        for h in heads:
            toeplitz = pltpu.roll(jnp.broadcast_to(row_ref[h], (hq, w)), 0, 1, stride=1, stride_axis=0)
            band = jnp.where(kc <= qc, jnp.where(kc >= qc - CA_LEFT_CHUNKS, toeplitz, NEG), NEG)
            tile = band[:, :3 * hq].T
            bias_ref[0, h] = tile
            bias_ref[1, h] = jnp.where(key_row >= bq, tile, NEG)
            bias_ref[2, h] = jnp.where(key_row >= hq, tile, NEG)

    variant = [jnp.where(qi == 0, 1 + j, 0) for j in range(2)]
    ones_rows = jnp.where(lax.broadcasted_iota(jnp.int32, (ONES_PAD, bq), 0) == 0, 1.0, 0.0).astype(BF16)

    sts = []
    for h in heads:
        kp, kc_ = kp_ref[:, hs[h]], kc_ref[:, hs[h]]
        windows = [jnp.concatenate([kp, kc_[:hq]], axis=0), jnp.concatenate([kp[hq:], kc_], axis=0)]
        for j in range(2):
            sts.append(lax.dot_general(windows[j], q_ref[j * hq:(j + 1) * hq, hs[h]], NT_DIMS,
                                       preferred_element_type=F32) + bias_ref[variant[j], h])
    for h in heads:
        vtp = jnp.concatenate([vp_ref[:, hs[h]].T, ones_rows], axis=0)
        vtc = jnp.concatenate([vc_ref[:, hs[h]].T, ones_rows], axis=0)
        windows = [jnp.concatenate([vtp, vtc[:, :hq]], axis=1), jnp.concatenate([vtp[:, hq:], vtc], axis=1)]
        for j in range(2):
            s = sts[2 * h + j]
            p = jnp.exp(s - jnp.max(s, axis=0, keepdims=True)).astype(BF16)
            o = jnp.dot(windows[j], p, preferred_element_type=F32)
            o_ref[j * hq:(j + 1) * hq, hs[h]] = (o[:d] / o[d:d + 1]).astype(o_ref.dtype).T


def _distance_rows(rel_bias, bq):
    heads, table = rel_bias.shape
    assert table == (CHUNK - 1) + REL_CLIP + 1 and bq >= REL_CLIP
    far = rel_bias[:, table - 1:]
    return jnp.concatenate([
        jnp.broadcast_to(far, (heads, bq - REL_CLIP)),
        rel_bias[:, ::-1],
        jnp.broadcast_to(far, (heads, bq - CHUNK)),
    ], axis=1).astype(F32).reshape(heads, 1, 2 * bq)


def chunk_attention(qkv, rel_bias, q_col, k_col, v_col, bq=512):
    b, s, _ = qkv.shape
    assert bq == CA_LEFT_CHUNKS * CHUNK
    rows = _distance_rows(rel_bias, bq)
    g = CA_GROUP
    w = g * HEAD_DIM
    assert q_col % g == 0 and k_col % g == 0 and v_col % g == 0
    cur = lambda col: pl.BlockSpec((None, bq, w), lambda h, bi, i: (bi, i, col // g + h))
    prev = lambda col: pl.BlockSpec((None, bq, w), lambda h, bi, i: (bi, jnp.maximum(i - 1, 0), col // g + h))
    return pl.pallas_call(
        functools.partial(_chunk_attn_kernel, bq=bq),
        grid=(HEADS // g, b, s // bq),
        in_specs=[cur(q_col), prev(k_col), cur(k_col), prev(v_col), cur(v_col),
                  pl.BlockSpec((g, 1, 2 * bq), lambda h, bi, i: (h, 0, 0))],
        out_specs=pl.BlockSpec((None, bq, w), lambda h, bi, i: (bi, i, h)),
        out_shape=jax.ShapeDtypeStruct((b, s, BRANCH), BF16),
        scratch_shapes=[pltpu.VMEM((3, g, 3 * bq // 2, bq // 2), F32)],
        compiler_params=_params("arbitrary", "arbitrary", "arbitrary"),
        name="chunk_attention",
    )(qkv, qkv, qkv, qkv, qkv, rows)


def _outproj_ln_kernel(a_ref, b_ref, wa_ref, wb_ref, x_ref, g_ref, beta_ref, o_ref):
    mix = (jnp.dot(a_ref[...], wa_ref[...], preferred_element_type=F32)
           + jnp.dot(b_ref[...], wb_ref[...], preferred_element_type=F32))
    o_ref[...] = _layer_norm_rows(ALPHA * x_ref[...] + mix, g_ref[...], beta_ref[...])


def outproj_ln(a, b, w_out, x, g, beta, bm=512):
    t, d = x.shape
    vec = pl.BlockSpec((1, d), lambda i: (0, 0))
    act = pl.BlockSpec((bm, BRANCH), lambda i: (i, 0))
    return pl.pallas_call(
        _outproj_ln_kernel,
        grid=(t // bm,),
        in_specs=[act, act,
                  pl.BlockSpec((BRANCH, d), lambda i: (0, 0)), pl.BlockSpec((BRANCH, d), lambda i: (1, 0)),
                  pl.BlockSpec((bm, d), lambda i: (i, 0)), vec, vec],
        out_specs=pl.BlockSpec((bm, d), lambda i: (i, 0)),
        out_shape=jax.ShapeDtypeStruct((t, d), F32),
        compiler_params=_params("parallel"),
        name="outproj_ln",
    )(a, b, w_out, w_out, x, g.reshape(1, d), beta.reshape(1, d))


def _mlp_ln_kernel(x_ref, w1_ref, w2_ref, g_ref, beta_ref, o_ref, xb_ref, acc_ref):
    j = pl.program_id(1)

    @pl.when(j == 0)
    def _():
        xb_ref[...] = x_ref[...].astype(BF16)
        acc_ref[...] = jnp.zeros_like(acc_ref)

    h = jnp.maximum(jnp.dot(xb_ref[...], w1_ref[...], preferred_element_type=F32), 0.0)
    acc_ref[...] += jnp.dot((h * h).astype(BF16), w2_ref[...], preferred_element_type=F32)

    @pl.when(j == pl.num_programs(1) - 1)
    def _():
        o_ref[...] = _layer_norm_rows(ALPHA * x_ref[...] + acc_ref[...], g_ref[...], beta_ref[...])


def mlp_ln(x, w1, w2, g, beta, bm=512, bf=1024):
    t, d = x.shape
    ff = w1.shape[1]
    vec = pl.BlockSpec((1, d), lambda i, j: (0, 0))
    return pl.pallas_call(
        _mlp_ln_kernel,
        grid=(t // bm, ff // bf),
        in_specs=[
            pl.BlockSpec((bm, d), lambda i, j: (i, 0)),
            pl.BlockSpec((d, bf), lambda i, j: (0, j)),
            pl.BlockSpec((bf, d), lambda i, j: (j, 0)),
            vec, vec,
        ],
        out_specs=pl.BlockSpec((bm, d), lambda i, j: (i, 0)),
        out_shape=jax.ShapeDtypeStruct((t, d), F32),
        scratch_shapes=[pltpu.VMEM((bm, d), BF16), pltpu.VMEM((bm, d), F32)],
        compiler_params=_params("parallel", "arbitrary"),
        name="mlp_ln",
    )(x, w1, w2, g.reshape(1, d), beta.reshape(1, d))


def even_mixer(x, w_in, conv_w, conv_b, conv_ln_g, conv_ln_b, lb_logits, slot, gnorm_g):
    b, s, d = x.shape
    n = w_in.shape[1]
    u = inproj(x.reshape(b * s, d), w_in, jnp.ones((1, n), F32), F32).reshape(b, s, n)
    a_out = conv_branch(u, conv_w, conv_b, conv_ln_g, conv_ln_b)
    b_out = hgrn2(u, lb_logits, slot, gnorm_g)
    return a_out.reshape(b * s, BRANCH), b_out.reshape(b * s, BRANCH)


def odd_mixer(x, w_in, b_f, rel_bias):
    b, s, d = x.shape
    w_qkv = jnp.concatenate([w_in[:, :3 * BRANCH], w_in[:, 3 * BRANCH + HEADS:]], axis=1).astype(BF16)
    w_f = w_in[:, 3 * BRANCH:3 * BRANCH + HEADS]
    scale = HEAD_DIM ** -0.5
    ones = jnp.ones((BRANCH,), F32)
    colscale = jnp.concatenate([ones * (scale * LOG2E), ones, ones, ones * scale, ones, ones]).reshape(1, 6 * BRANCH)
    qkv = inproj(x.reshape(b * s, d), w_qkv, colscale, BF16).reshape(b, s, 6 * BRANCH)
    fcum = forget_cumsum(x, w_f, b_f)
    c_out = fox_attention(qkv, fcum, 0, HEADS, 2 * HEADS)
    d_out = chunk_attention(qkv, rel_bias, 3 * HEADS, 4 * HEADS, 5 * HEADS)
    return c_out.reshape(b * s, BRANCH), d_out.reshape(b * s, BRANCH)


def kernel(x, ev_w_in, ev_conv_w, ev_conv_b, ev_conv_ln_g, ev_conv_ln_b, hgrn_lb_logits, ev_gnorm_g, ev_w_out, od_w_in, fox_b_f, rel_bias, od_w_out, ln_mix_g, ln_mix_b, mlp_w1, mlp_w2, ln_mlp_g, ln_mlp_b):
    b, s, d = x.shape
    for l in range(DEPTH):
        j = l // 2
        if l % 2 == 0:
            p, r = even_mixer(x, cast_layer_bf16(ev_w_in, j), ev_conv_w[j], ev_conv_b[j], ev_conv_ln_g[j],
                              ev_conv_ln_b[j], hgrn_lb_logits, j, ev_gnorm_g[j])
            w_out = cast_layer_bf16(ev_w_out, j)
        else:
            p, r = odd_mixer(x, od_w_in[j], fox_b_f[j], rel_bias[j])
            w_out = cast_layer_bf16(od_w_out, j)
        x2 = outproj_ln(p, r, w_out, x.reshape(b * s, d), ln_mix_g[l], ln_mix_b[l])
        x2 = mlp_ln(x2, cast_layer_bf16(mlp_w1, l), cast_layer_bf16(mlp_w2, l), ln_mlp_g[l], ln_mlp_b[l])
        x = x2.reshape(b, s, d)
    return x
```

```python
import functools

import jax
import jax.numpy as jnp
from jax import lax
from jax.experimental import pallas as pl
from jax.experimental.pallas import tpu as pltpu

F32 = jnp.float32
BF16 = jnp.bfloat16

DEPTH = 2
CHUNK = 64
LN_EPS = 1e-5
ALPHA = (2 * DEPTH) ** 0.25
HEADS = 8
HEAD_DIM = 128
BRANCH = HEADS * HEAD_DIM
CONV_WIDTH = 31
CONV_HALO = 32
CA_LEFT_CHUNKS = 8
REL_CLIP = 256
NEG = -1e30
LOG2E = 1.4426950408889634
F_TERMS = 3

ONES_PAD = 16
LANES = 128
SUBLANES = 8
VMEM_LIMIT = 56 * 1024 * 1024

NT_DIMS = (((1,), (1,)), ((), ()))
TN_DIMS = (((0,), (0,)), ((), ()))


def _params(*sem):
    return pltpu.CompilerParams(dimension_semantics=sem, vmem_limit_bytes=VMEM_LIMIT)


def _sigmoid(x):
    return 1.0 / (1.0 + jnp.exp(-x))


def _layer_norm_rows(y, g, b):
    mu = jnp.mean(y, axis=-1, keepdims=True)
    d = y - mu
    var = jnp.mean(d * d, axis=-1, keepdims=True)
    return d * lax.rsqrt(var + LN_EPS) * g + b


def _split3(x):
    hi = x.astype(BF16)
    r = x - hi.astype(F32)
    mid = r.astype(BF16)
    lo = (r - mid.astype(F32)).astype(BF16)
    return hi, mid, lo


def _cast_kernel(w_ref, o_ref):
    o_ref[...] = w_ref[...].astype(o_ref.dtype)


def cast_layer_bf16(w, layer, br=512, bc=2048):
    _, r, c = w.shape
    br, bc = min(br, r), min(bc, c)
    assert r % br == 0 and c % bc == 0
    return pl.pallas_call(
        _cast_kernel,
        grid=(r // br, c // bc),
        in_specs=[pl.BlockSpec((None, br, bc), lambda i, j: (layer, i, j))],
        out_specs=pl.BlockSpec((br, bc), lambda i, j: (i, j)),
        out_shape=jax.ShapeDtypeStruct((r, c), BF16),
        compiler_params=_params("parallel", "parallel"),
        name="cast_bf16",
    )(w)


def _inproj_kernel(x_ref, w_ref, s_ref, o_ref, xb_ref):
    @pl.when(pl.program_id(1) == 0)
    def _():
        xb_ref[...] = x_ref[...].astype(BF16)

    acc = jnp.dot(xb_ref[...], w_ref[...], preferred_element_type=F32)
    o_ref[...] = (acc * s_ref[...]).astype(o_ref.dtype)


def inproj(x, w, colscale, out_dtype, bm=1024, bn=1024):
    t, k = x.shape
    n = w.shape[1]
    assert t % bm == 0 and n % bn == 0
    return pl.pallas_call(
        _inproj_kernel,
        grid=(t // bm, n // bn),
        in_specs=[
            pl.BlockSpec((bm, k), lambda i, j: (i, 0)),
            pl.BlockSpec((k, bn), lambda i, j: (0, j)),
            pl.BlockSpec((1, bn), lambda i, j: (0, j)),
        ],
        out_specs=pl.BlockSpec((bm, bn), lambda i, j: (i, j)),
        out_shape=jax.ShapeDtypeStruct((t, n), out_dtype),
        scratch_shapes=[pltpu.VMEM((bm, k), BF16)],
        compiler_params=_params("parallel", "arbitrary"),
        name="inproj",
    )(x, w, colscale)


CONV_ROWS = 64


def _conv_kernel(a_ref, g_ref, ah_ref, gh_ref, w_ref, cb_ref, lg_ref, lb_ref, o_ref, hext_ref, hs_ref, y_ref, wb_ref,
                 *, bt):
    i = pl.program_id(1)
    halo = ah_ref[...] * _sigmoid(gh_ref[...])
    hext_ref[0:CONV_HALO, :] = jnp.where(i > 0, halo, 0.0)
    hext_ref[CONV_HALO:CONV_HALO + bt, :] = a_ref[...] * _sigmoid(g_ref[...])
    span = bt + CONV_HALO - SUBLANES
    for d in range(1, SUBLANES):
        hs_ref[d - 1, 0:span, :] = hext_ref[d:d + span, :]

    first = CONV_HALO - (CONV_WIDTH - 1)
    for j in range(CONV_WIDTH):
        wb_ref[j] = jnp.broadcast_to(w_ref[j:j + 1, :], (SUBLANES, BRANCH))
    wb_ref[CONV_WIDTH] = jnp.broadcast_to(cb_ref[...], (SUBLANES, BRANCH))
    groups = CONV_ROWS // SUBLANES

    def rows(c, carry):
        base = pl.multiple_of(c * CONV_ROWS, CONV_ROWS)
        for cg in range(BRANCH // 256):
            cs = slice(cg * 256, (cg + 1) * 256)
            acc = jnp.broadcast_to(wb_ref[CONV_WIDTH, :, cs][None], (groups, SUBLANES, 256))
            for j in range(CONV_WIDTH):
                off = first + j
                d, al = off % SUBLANES, off - off % SUBLANES
                if d == 0:
                    tap = hext_ref[pl.ds(base + al, CONV_ROWS), cs]
                else:
                    tap = hs_ref[d - 1, pl.ds(base + al, CONV_ROWS), cs]
                acc = acc + wb_ref[j, :, cs][None] * tap.reshape(groups, SUBLANES, 256)
            y_ref[pl.ds(base, CONV_ROWS), cs] = acc.reshape(CONV_ROWS, 256)
        return carry

    lax.fori_loop(0, bt // CONV_ROWS, rows, 0)
    z = _layer_norm_rows(y_ref[...], lg_ref[...], lb_ref[...])
    o_ref[...] = (z * _sigmoid(z)).astype(o_ref.dtype)


def conv_branch(u, conv_w, conv_b, ln_g, ln_b, bt=512):
    b, s, _ = u.shape
    assert s % bt == 0 and bt % CONV_ROWS == 0 and bt % CONV_HALO == 0
    hb = bt // CONV_HALO
    w = jnp.pad(conv_w, ((0, CONV_HALO - CONV_WIDTH), (0, 0)))
    row = lambda v: v.reshape(1, BRANCH)
    vec = pl.BlockSpec((1, BRANCH), lambda bi, i: (0, 0))
    return pl.pallas_call(
        functools.partial(_conv_kernel, bt=bt),
        grid=(b, s // bt),
        in_specs=[
            pl.BlockSpec((None, bt, BRANCH), lambda bi, i: (bi, i, 0)),
            pl.BlockSpec((None, bt, BRANCH), lambda bi, i: (bi, i, 1)),
            pl.BlockSpec((None, CONV_HALO, BRANCH), lambda bi, i: (bi, jnp.maximum(i * hb - 1, 0), 0)),
            pl.BlockSpec((None, CONV_HALO, BRANCH), lambda bi, i: (bi, jnp.maximum(i * hb - 1, 0), 1)),
            pl.BlockSpec((CONV_HALO, BRANCH), lambda bi, i: (0, 0)),
            vec, vec, vec,
        ],
        out_specs=pl.BlockSpec((None, bt, BRANCH), lambda bi, i: (bi, i, 0)),
        out_shape=jax.ShapeDtypeStruct((b, s, BRANCH), BF16),
        scratch_shapes=[
            pltpu.VMEM((bt + CONV_HALO, BRANCH), F32),
            pltpu.VMEM((SUBLANES - 1, bt + CONV_HALO - SUBLANES, BRANCH), F32),
            pltpu.VMEM((bt, BRANCH), F32),
            pltpu.VMEM((CONV_WIDTH + 1, SUBLANES, BRANCH), F32),
        ],
        compiler_params=_params("parallel", "arbitrary"),
        name="conv_branch",
    )(u, u, u, u, w, row(conv_b), row(ln_g), row(ln_b))


HG_HALF = CHUNK // 2
HG_PAIR = 2 * HEAD_DIM
HG_STEPS = 4


def _hgrn_kernel(q_ref, f_ref, v_ref, g_ref, lbl_ref, gn_ref, o_ref, state_ref, *, slot):
    @pl.when(pl.program_id(1) == 0)
    def _():
        state_ref[...] = jnp.zeros_like(state_ref)

    c, hh = CHUNK, HG_HALF
    lbl = lbl_ref[...]
    e = jnp.exp(lbl - jnp.max(lbl, axis=0, keepdims=True))
    upto = lax.broadcasted_iota(jnp.int32, e.shape, 0) <= slot
    lb = jnp.sum(jnp.where(upto, e, 0.0), axis=0, keepdims=True) / jnp.sum(e, axis=0, keepdims=True)
    r_i = lax.broadcasted_iota(jnp.int32, (c, c), 0)
    c_i = lax.broadcasted_iota(jnp.int32, (c, c), 1)
    tri = jnp.where(c_i <= r_i, 1.0, 0.0).astype(BF16)

    def gates(rows):
        f = lb + (1.0 - lb) * _sigmoid(f_ref[rows, :])
        lf = jnp.log(f)
        kk = 1.0 - f
        q = q_ref[rows, :]
        qs = q * _sigmoid(q)
        l3 = jnp.dot(tri, jnp.concatenate(_split3(lf), axis=1), preferred_element_type=F32)
        ll = l3[:, :BRANCH] + l3[:, BRANCH:2 * BRANCH] + l3[:, 2 * BRANCH:]
        l_mid0 = ll[hh // 2 - 1:hh // 2]
        l_edge = ll[hh - 1:hh]
        l_mid1 = ll[hh + hh // 2 - 1:hh + hh // 2]
        l_end = ll[c - 1:c]
        lt, lbt = ll[:hh], ll[hh:]
        return dict(
            a_d0=qs[:hh] * jnp.exp(lt - l_mid0),
            b_d0=kk[:hh] * jnp.exp(l_mid0 - lt),
            a_off=qs[hh:] * jnp.exp(lbt - l_edge),
            b_off=kk[:hh] * jnp.exp(l_edge - lt),
            a_d1=qs[hh:] * jnp.exp(lbt - l_mid1),
            b_d1=kk[hh:] * jnp.exp(l_mid1 - lbt),
            a_int=(qs * jnp.exp(ll)).astype(BF16),
            kd=(kk * jnp.exp(l_end - ll)).astype(BF16),
            dec=jnp.exp(l_end),
        )

    zero = jnp.zeros((hh, HG_PAIR), F32)
    first = lax.broadcasted_iota(jnp.int32, (c, HG_PAIR), 1) < HEAD_DIM
    first3 = jnp.concatenate([first, first, first], axis=1)
    row2 = lax.broadcasted_iota(jnp.int32, (HG_PAIR, HG_PAIR), 0) < HEAD_DIM
    col2 = lax.broadcasted_iota(jnp.int32, (HG_PAIR, HG_PAIR), 1) < HEAD_DIM
    same_head = row2 == col2
    key_i = lax.broadcasted_iota(jnp.int32, (c, 2 * c), 1) % c
    qry_i = lax.broadcasted_iota(jnp.int32, (c, 2 * c), 0)
    stack = lambda top, bottom: jnp.concatenate([top, bottom], axis=0)

    pairs = range(HEADS // 2)
    pss = [slice(p * HG_PAIR, (p + 1) * HG_PAIR) for p in pairs]

    def scores(gt):
        scs = []
        for ps in pss:
            lhs = jnp.concatenate([stack(gt["a_d0"][:, ps], zero), stack(zero, gt["a_off"][:, ps]),
                                   stack(zero, gt["a_d1"][:, ps])], axis=1)
            keys = jnp.concatenate([stack(gt["b_d0"][:, ps], zero), stack(gt["b_off"][:, ps], zero),
                                    stack(zero, gt["b_d1"][:, ps])], axis=1)
            rhs_t = stack(jnp.where(first3, keys, 0.0), jnp.where(first3, 0.0, keys))
            sc = lax.dot_general(lhs.astype(BF16), rhs_t.astype(BF16), NT_DIMS, preferred_element_type=F32)
            scs.append(jnp.where(key_i <= qry_i, sc, 0.0))
        return scs

    chunks = [slice(k * c, (k + 1) * c) for k in range(o_ref.shape[0] // c)]
    gts = [gates(rows) for rows in chunks]
    scss = [scores(gt) for gt in gts]
    states = [state_ref[p] for p in pairs]
    for rows, gt, scs in zip(chunks, gts, scss):
        inter = []
        for p in pairs:
            st = states[p]
            inter.append(lax.dot_general(gt["a_int"][:, pss[p]], st.astype(BF16), NT_DIMS, preferred_element_type=F32))
            upd = lax.dot_general(v_ref[rows, pss[p]].astype(BF16), gt["kd"][:, pss[p]], TN_DIMS,
                                  preferred_element_type=F32)
            states[p] = gt["dec"][:, pss[p]] * st + jnp.where(same_head, upd, 0.0)
        for p in pairs:
            v = v_ref[rows, pss[p]]
            v_bd = stack(jnp.where(first, v, 0.0), jnp.where(first, 0.0, v)).astype(BF16)
            o = jnp.dot(scs[p].astype(BF16), v_bd, preferred_element_type=F32) + inter[p]
            for j in range(2):
                hsl = slice(p * HG_PAIR + j * HEAD_DIM, p * HG_PAIR + (j + 1) * HEAD_DIM)
                oj = o[:, j * HEAD_DIM:(j + 1) * HEAD_DIM]
                oj = oj * lax.rsqrt(jnp.mean(oj * oj, axis=-1, keepdims=True) + LN_EPS)
                g = g_ref[rows, hsl]
                o_ref[rows, hsl] = (oj * gn_ref[:, hsl] * (g * _sigmoid(g))).astype(o_ref.dtype)
    for p in pairs:
        state_ref[p] = states[p]


def hgrn2(u, lb_logits, slot, gnorm_g):
    b, s, _ = u.shape
    slots = lb_logits.shape[0]
    rows = HG_STEPS * CHUNK
    assert s % rows == 0
    col = lambda j: pl.BlockSpec((None, rows, BRANCH), lambda bi, i: (bi, i, j))
    vec = pl.BlockSpec((1, BRANCH), lambda bi, i: (0, 0))
    return pl.pallas_call(
        functools.partial(_hgrn_kernel, slot=slot),
        grid=(b, s // rows),
        in_specs=[col(2), col(3), col(4), col(5), pl.BlockSpec((slots, BRANCH), lambda bi, i: (0, 0)), vec],
        out_specs=pl.BlockSpec((None, rows, BRANCH), lambda bi, i: (bi, i, 0)),
        out_shape=jax.ShapeDtypeStruct((b, s, BRANCH), BF16),
        scratch_shapes=[pltpu.VMEM((HEADS // 2, HG_PAIR, HG_PAIR), F32)],
        compiler_params=_params("parallel", "arbitrary"),
        name="hgrn2",
    )(u, u, u, u, lb_logits.astype(F32), gnorm_g.reshape(1, BRANCH))


def _fgate_kernel(x_ref, w_ref, b_ref, o_ref, carry_ref, *, bm):
    @pl.when(pl.program_id(1) == 0)
    def _():
        carry_ref[...] = jnp.zeros_like(carry_ref)

    z = jnp.dot(x_ref[...].astype(BF16), w_ref[...], preferred_element_type=F32) + b_ref[...]
    ls = jnp.minimum(z, 0.0) - jnp.log(1.0 + jnp.exp(-jnp.abs(z)))
    r_i = lax.broadcasted_iota(jnp.int32, (bm, bm), 0)
    c_i = lax.broadcasted_iota(jnp.int32, (bm, bm), 1)
    tri = jnp.where(c_i <= r_i, 1.0, 0.0).astype(BF16)
    hi, mid, lo = _split3(ls)
    cum = (jnp.dot(tri, hi, preferred_element_type=F32)
           + jnp.dot(tri, mid, preferred_element_type=F32)
           + jnp.dot(tri, lo, preferred_element_type=F32)) + carry_ref[...]
    carry_ref[...] = cum[bm - 1:bm, :]
    neg = cum * (-LOG2E)
    lane = lax.broadcasted_iota(jnp.int32, (bm, LANES), 1)
    for h in range(HEADS):
        hi, mid, lo = (t.astype(F32) for t in _split3(jnp.broadcast_to(neg[:, h:h + 1], (bm, LANES))))
        o_ref[h] = jnp.where(lane == 0, hi, jnp.where(lane == 1, mid, jnp.where(lane == 2, lo, 0.0))).astype(BF16)


def forget_cumsum(x, w_f, b_f, bm=512):
    b, s, d = x.shape
    assert s % bm == 0
    w = jnp.pad(w_f, ((0, 0), (0, LANES - HEADS))).astype(BF16)
    bias = jnp.pad(b_f, (0, LANES - HEADS)).reshape(1, LANES).astype(F32)
    return pl.pallas_call(
        functools.partial(_fgate_kernel, bm=bm),
        grid=(b, s // bm),
        in_specs=[
            pl.BlockSpec((None, bm, d), lambda bi, i: (bi, i, 0)),
            pl.BlockSpec((d, LANES), lambda bi, i: (0, 0)),
            pl.BlockSpec((1, LANES), lambda bi, i: (0, 0)),
        ],
        out_specs=pl.BlockSpec((None, HEADS, bm, LANES), lambda bi, i: (bi, 0, i, 0)),
        out_shape=jax.ShapeDtypeStruct((b, HEADS, s, LANES), BF16),
        scratch_shapes=[pltpu.VMEM((1, LANES), F32)],
        compiler_params=_params("parallel", "arbitrary"),
        name="forget_cumsum",
    )(x, w, bias)


FOX_GROUP = 4


def _fox_kernel(q_ref, k_ref, v_ref, f_ref, o_ref, vt_ref, acc_ref, *, bq):
    qi = pl.program_id(2)
    nblk = pl.num_programs(2)
    d = HEAD_DIM
    heads = range(FOX_GROUP)
    hs = [slice(h * d, (h + 1) * d) for h in heads]

    @pl.when(qi == 0)
    def _():
        ones_row = jnp.where(lax.broadcasted_iota(jnp.int32, (ONES_PAD, bq), 0) == 0, 1.0, 0.0).astype(BF16)

        def tr(c, carry):
            cs = pl.multiple_of(c * bq, bq)
            for h in heads:
                vt_ref[h, :d, pl.ds(cs, bq)] = v_ref[pl.ds(cs, bq), hs[h]].T
                vt_ref[h, d:, pl.ds(cs, bq)] = ones_row
            return carry
        lax.fori_loop(0, nblk, tr, 0)

    acc_ref[...] = jnp.zeros_like(acc_ref)

    ones3 = jnp.where(lax.broadcasted_iota(jnp.int32, (bq, LANES), 1) < F_TERMS, 1.0, 0.0).astype(BF16)
    q_aug = [jnp.concatenate([q_ref[:, hs[h]], ones3], axis=1) for h in heads]

    def step(ks, nk, carry, causal_from):
        sts = []
        for h in heads:
            k_aug = jnp.concatenate([k_ref[pl.ds(ks, nk), hs[h]], f_ref[h, pl.ds(ks, nk), :]], axis=1)
            sts.append(lax.dot_general(k_aug, q_aug[h], NT_DIMS, preferred_element_type=F32))
        if causal_from is not None:
            keep = (lax.broadcasted_iota(jnp.int32, (nk, bq), 0) - causal_from
                    <= lax.broadcasted_iota(jnp.int32, (nk, bq), 1))
            sts = [jnp.where(keep, st, NEG) for st in sts]
        out = []
        for h in heads:
            m = carry[h]
            m_new = jnp.maximum(m, jnp.max(sts[h], axis=0, keepdims=True))
            p = jnp.exp2(sts[h] - m_new).astype(BF16)
            pv = jnp.dot(vt_ref[h, :, pl.ds(ks, nk)], p, preferred_element_type=F32)
            acc_ref[h] = jnp.exp2(m - m_new) * acc_ref[h] + pv
            out.append(m_new)
        return tuple(out)

    def finish(ks, nk, carry, causal_from):
        step(ks, nk, carry, causal_from)
        for h in heads:
            o_ref[:, hs[h]] = (acc_ref[h, :d, :] / acc_ref[h, d:d + 1, :]).astype(o_ref.dtype).T

    init = tuple(jnp.full((1, bq), NEG, F32) for _ in heads)
    carry = lax.fori_loop(0, qi // 2, lambda t, cr: step(pl.multiple_of(t * 2 * bq, 2 * bq), 2 * bq, cr, None), init)

    @pl.when(qi % 2 == 0)
    def _():
        finish(pl.multiple_of(qi * bq, bq), bq, carry, 0)

    @pl.when(qi % 2 == 1)
    def _():
        finish(pl.multiple_of((qi - 1) * bq, 2 * bq), 2 * bq, carry, bq)


def fox_attention(qkv, fcum, q_col, k_col, v_col, bq=512):
    b, s, _ = qkv.shape
    g = FOX_GROUP
    w = g * HEAD_DIM
    assert q_col % g == 0 and k_col % g == 0 and v_col % g == 0 and HEADS % g == 0 and s % bq == 0
    return pl.pallas_call(
        functools.partial(_fox_kernel, bq=bq),
        grid=(b, HEADS // g, s // bq),
        in_specs=[
            pl.BlockSpec((None, bq, w), lambda bi, h, i: (bi, i, q_col // g + h)),
            pl.BlockSpec((None, s, w), lambda bi, h, i: (bi, 0, k_col // g + h), pipeline_mode=pl.Buffered(1)),
            pl.BlockSpec((None, s, w), lambda bi, h, i: (bi, 0, v_col // g + h), pipeline_mode=pl.Buffered(1)),
            pl.BlockSpec((None, g, s, LANES), lambda bi, h, i: (bi, h, 0, 0), pipeline_mode=pl.Buffered(1)),
        ],
        out_specs=pl.BlockSpec((None, bq, w), lambda bi, h, i: (bi, i, h)),
        out_shape=jax.ShapeDtypeStruct((b, s, BRANCH), BF16),
        scratch_shapes=[pltpu.VMEM((g, HEAD_DIM + ONES_PAD, s), BF16), pltpu.VMEM((g, HEAD_DIM + ONES_PAD, bq), F32)],
        compiler_params=_params("parallel", "parallel", "arbitrary"),
        name="fox_attention",
    )(qkv, qkv, qkv, fcum)


CA_GROUP = 4


def _chunk_attn_kernel(q_ref, kp_ref, kc_ref, vp_ref, vc_ref, row_ref, o_ref, bias_ref, *, bq):
    qi = pl.program_id(2)
    d = HEAD_DIM
    hq = bq // 2
    heads = range(CA_GROUP)
    hs = [slice(h * d, (h + 1) * d) for h in heads]

    @pl.when((pl.program_id(1) == 0) & (qi == 0))
    def _():
        w = 2 * bq
        qc = lax.broadcasted_iota(jnp.int32, (hq, w), 0) // CHUNK
        kc = lax.broadcasted_iota(jnp.int32, (hq, w), 1) // CHUNK - bq // CHUNK
        key_row = lax.broadcasted_iota(jnp.int32, (3 * hq, hq), 0)
        for h in heads:
            toeplitz = pltpu.roll(jnp.broadcast_to(row_ref[h], (hq, w)), 0, 1, stride=1, stride_axis=0)
            band = jnp.where(kc <= qc, jnp.where(kc >= qc - CA_LEFT_CHUNKS, toeplitz, NEG), NEG)
            tile = band[:, :3 * hq].T
            bias_ref[0, h] = tile
            bias_ref[1, h] = jnp.where(key_row >= bq, tile, NEG)
            bias_ref[2, h] = jnp.where(key_row >= hq, tile, NEG)

    variant = [jnp.where(qi == 0, 1 + j, 0) for j in range(2)]
    ones_rows = jnp.where(lax.broadcasted_iota(jnp.int32, (ONES_PAD, bq), 0) == 0, 1.0, 0.0).astype(BF16)

    sts = []
    for h in heads:
        kp, kc_ = kp_ref[:, hs[h]], kc_ref[:, hs[h]]
        windows = [jnp.concatenate([kp, kc_[:hq]], axis=0), jnp.concatenate([kp[hq:], kc_], axis=0)]
        for j in range(2):
            sts.append(lax.dot_general(windows[j], q_ref[j * hq:(j + 1) * hq, hs[h]], NT_DIMS,
                                       preferred_element_type=F32) + bias_ref[variant[j], h])
    for h in heads:
        vtp = jnp.concatenate([vp_ref[:, hs[h]].T, ones_rows], axis=0)
        vtc = jnp.concatenate([vc_ref[:, hs[h]].T, ones_rows], axis=0)
        windows = [jnp.concatenate([vtp, vtc[:, :hq]], axis=1), jnp.concatenate([vtp[:, hq:], vtc], axis=1)]
        for j in range(2):
            s = sts[2 * h + j]
            p = jnp.exp(s - jnp.max(s, axis=0, keepdims=True)).astype(BF16)
            o = jnp.dot(windows[j], p, preferred_element_type=F32)
            o_ref[j * hq:(j + 1) * hq, hs[h]] = (o[:d] / o[d:d + 1]).astype(o_ref.dtype).T


def _distance_rows(rel_bias, bq):
    heads, table = rel_bias.shape
    assert table == (CHUNK - 1) + REL_CLIP + 1 and bq >= REL_CLIP
    far = rel_bias[:, table - 1:]
    return jnp.concatenate([
        jnp.broadcast_to(far, (heads, bq - REL_CLIP)),
        rel_bias[:, ::-1],
        jnp.broadcast_to(far, (heads, bq - CHUNK)),
    ], axis=1).astype(F32).reshape(heads, 1, 2 * bq)


def chunk_attention(qkv, rel_bias, q_col, k_col, v_col, bq=512):
    b, s, _ = qkv.shape
    assert bq == CA_LEFT_CHUNKS * CHUNK
    rows = _distance_rows(rel_bias, bq)
    g = CA_GROUP
    w = g * HEAD_DIM
    assert q_col % g == 0 and k_col % g == 0 and v_col % g == 0 and HEADS % g == 0 and s % bq == 0
    cur = lambda col: pl.BlockSpec((None, bq, w), lambda h, bi, i: (bi, i, col // g + h))
    prev = lambda col: pl.BlockSpec((None, bq, w), lambda h, bi, i: (bi, jnp.maximum(i - 1, 0), col // g + h))
    return pl.pallas_call(
        functools.partial(_chunk_attn_kernel, bq=bq),
        grid=(HEADS // g, b, s // bq),
        in_specs=[cur(q_col), prev(k_col), cur(k_col), prev(v_col), cur(v_col),
                  pl.BlockSpec((g, 1, 2 * bq), lambda h, bi, i: (h, 0, 0))],
        out_specs=pl.BlockSpec((None, bq, w), lambda h, bi, i: (bi, i, h)),
        out_shape=jax.ShapeDtypeStruct((b, s, BRANCH), BF16),
        scratch_shapes=[pltpu.VMEM((3, g, 3 * bq // 2, bq // 2), F32)],
        compiler_params=_params("arbitrary", "arbitrary", "arbitrary"),
        name="chunk_attention",
    )(qkv, qkv, qkv, qkv, qkv, rows)


def _outproj_ln_kernel(a_ref, b_ref, wa_ref, wb_ref, x_ref, g_ref, beta_ref, o_ref):
    mix = (jnp.dot(a_ref[...], wa_ref[...], preferred_element_type=F32)
           + jnp.dot(b_ref[...], wb_ref[...], preferred_element_type=F32))
    o_ref[...] = _layer_norm_rows(ALPHA * x_ref[...] + mix, g_ref[...], beta_ref[...])


def outproj_ln(a, b, w_out, x, g, beta, bm=512):
    t, d = x.shape
    assert t % bm == 0 and w_out.shape == (2 * BRANCH, d)
    vec = pl.BlockSpec((1, d), lambda i: (0, 0))
    act = pl.BlockSpec((bm, BRANCH), lambda i: (i, 0))
    return pl.pallas_call(
        _outproj_ln_kernel,
        grid=(t // bm,),
        in_specs=[act, act,
                  pl.BlockSpec((BRANCH, d), lambda i: (0, 0)), pl.BlockSpec((BRANCH, d), lambda i: (1, 0)),
                  pl.BlockSpec((bm, d), lambda i: (i, 0)), vec, vec],
        out_specs=pl.BlockSpec((bm, d), lambda i: (i, 0)),
        out_shape=jax.ShapeDtypeStruct((t, d), F32),
        compiler_params=_params("parallel"),
        name="outproj_ln",
    )(a, b, w_out, w_out, x, g.reshape(1, d), beta.reshape(1, d))


def _mlp_ln_kernel(x_ref, w1_ref, w2_ref, g_ref, beta_ref, o_ref, xb_ref, acc_ref):
    j = pl.program_id(1)

    @pl.when(j == 0)
    def _():
        xb_ref[...] = x_ref[...].astype(BF16)
        acc_ref[...] = jnp.zeros_like(acc_ref)

    h = jnp.maximum(jnp.dot(xb_ref[...], w1_ref[...], preferred_element_type=F32), 0.0)
    acc_ref[...] += jnp.dot((h * h).astype(BF16), w2_ref[...], preferred_element_type=F32)

    @pl.when(j == pl.num_programs(1) - 1)
    def _():
        o_ref[...] = _layer_norm_rows(ALPHA * x_ref[...] + acc_ref[...], g_ref[...], beta_ref[...])


def mlp_ln(x, w1, w2, g, beta, bm=512, bf=1024):
    t, d = x.shape
    ff = w1.shape[1]
    assert t % bm == 0 and ff % bf == 0
    vec = pl.BlockSpec((1, d), lambda i, j: (0, 0))
    return pl.pallas_call(
        _mlp_ln_kernel,
        grid=(t // bm, ff // bf),
        in_specs=[
            pl.BlockSpec((bm, d), lambda i, j: (i, 0)),
            pl.BlockSpec((d, bf), lambda i, j: (0, j)),
            pl.BlockSpec((bf, d), lambda i, j: (j, 0)),
            vec, vec,
        ],
        out_specs=pl.BlockSpec((bm, d), lambda i, j: (i, 0)),
        out_shape=jax.ShapeDtypeStruct((t, d), F32),
        scratch_shapes=[pltpu.VMEM((bm, d), BF16), pltpu.VMEM((bm, d), F32)],
        compiler_params=_params("parallel", "arbitrary"),
        name="mlp_ln",
    )(x, w1, w2, g.reshape(1, d), beta.reshape(1, d))


def even_mixer(x, w_in, conv_w, conv_b, conv_ln_g, conv_ln_b, lb_logits, slot, gnorm_g):
    b, s, d = x.shape
    n = w_in.shape[1]
    u = inproj(x.reshape(b * s, d), w_in, jnp.ones((1, n), F32), F32).reshape(b, s, n)
    a_out = conv_branch(u, conv_w, conv_b, conv_ln_g, conv_ln_b)
    b_out = hgrn2(u, lb_logits, slot, gnorm_g)
    return a_out.reshape(b * s, BRANCH), b_out.reshape(b * s, BRANCH)


def odd_mixer(x, w_in, b_f, rel_bias):
    b, s, d = x.shape
    w_qkv = jnp.concatenate([w_in[:, :3 * BRANCH], w_in[:, 3 * BRANCH + HEADS:]], axis=1).astype(BF16)
    w_f = w_in[:, 3 * BRANCH:3 * BRANCH + HEADS]
    scale = HEAD_DIM ** -0.5
    ones = jnp.ones((BRANCH,), F32)
    colscale = jnp.concatenate([ones * (scale * LOG2E), ones, ones, ones * scale, ones, ones]).reshape(1, 6 * BRANCH)
    qkv = inproj(x.reshape(b * s, d), w_qkv, colscale, BF16).reshape(b, s, 6 * BRANCH)
    fcum = forget_cumsum(x, w_f, b_f)
    c_out = fox_attention(qkv, fcum, 0, HEADS, 2 * HEADS)
    d_out = chunk_attention(qkv, rel_bias, 3 * HEADS, 4 * HEADS, 5 * HEADS)
    return c_out.reshape(b * s, BRANCH), d_out.reshape(b * s, BRANCH)


def kernel(x, ev_w_in, ev_conv_w, ev_conv_b, ev_conv_ln_g, ev_conv_ln_b, hgrn_lb_logits, ev_gnorm_g, ev_w_out, od_w_in, fox_b_f, rel_bias, od_w_out, ln_mix_g, ln_mix_b, mlp_w1, mlp_w2, ln_mlp_g, ln_mlp_b):
    b, s, d = x.shape
    for l in range(DEPTH):
        j = l // 2
        if l % 2 == 0:
            p, r = even_mixer(x, cast_layer_bf16(ev_w_in, j), ev_conv_w[j], ev_conv_b[j], ev_conv_ln_g[j],
                              ev_conv_ln_b[j], hgrn_lb_logits, j, ev_gnorm_g[j])
            w_out = cast_layer_bf16(ev_w_out, j)
        else:
            p, r = odd_mixer(x, od_w_in[j], fox_b_f[j], rel_bias[j])
            w_out = cast_layer_bf16(od_w_out, j)
        x2 = outproj_ln(p, r, w_out, x.reshape(b * s, d), ln_mix_g[l], ln_mix_b[l])
        x2 = mlp_ln(x2, cast_layer_bf16(mlp_w1, l), cast_layer_bf16(mlp_w2, l), ln_mlp_g[l], ln_mlp_b[l])
        x = x2.reshape(b, s, d)
    return x
```

```python
import functools

import jax
import jax.numpy as jnp
from jax import lax
from jax.experimental import pallas as pl
from jax.experimental.pallas import tpu as pltpu

F32 = jnp.float32
BF16 = jnp.bfloat16

DEPTH = 2
CHUNK = 64
LN_EPS = 1e-5
ALPHA = (2 * DEPTH) ** 0.25
HEADS = 8
HEAD_DIM = 128
BRANCH = HEADS * HEAD_DIM
CONV_WIDTH = 31
CONV_HALO = 32
CA_LEFT_CHUNKS = 8
REL_CLIP = 256
NEG = -1e30
LOG2E = 1.4426950408889634
F_TERMS = 3

ONES_PAD = 16
LANES = 128
SUBLANES = 8
VMEM_LIMIT = 56 * 1024 * 1024

NT_DIMS = (((1,), (1,)), ((), ()))
TN_DIMS = (((0,), (0,)), ((), ()))


def _params(*sem):
    return pltpu.CompilerParams(dimension_semantics=sem, vmem_limit_bytes=VMEM_LIMIT)


def _sigmoid(x):
    return 1.0 / (1.0 + jnp.exp(-x))


def _layer_norm_rows(y, g, b):
    mu = jnp.mean(y, axis=-1, keepdims=True)
    d = y - mu
    var = jnp.mean(d * d, axis=-1, keepdims=True)
    return d * lax.rsqrt(var + LN_EPS) * g + b


def _split3(x):
    hi = x.astype(BF16)
    r = x - hi.astype(F32)
    mid = r.astype(BF16)
    lo = (r - mid.astype(F32)).astype(BF16)
    return hi, mid, lo


def _cast_kernel(w_ref, o_ref):
    o_ref[...] = w_ref[...].astype(o_ref.dtype)


def cast_layer_bf16(w, layer, br=512, bc=2048):
    _, r, c = w.shape
    br, bc = min(br, r), min(bc, c)
    assert r % br == 0 and c % bc == 0
    return pl.pallas_call(
        _cast_kernel,
        grid=(r // br, c // bc),
        in_specs=[pl.BlockSpec((None, br, bc), lambda i, j: (layer, i, j))],
        out_specs=pl.BlockSpec((br, bc), lambda i, j: (i, j)),
        out_shape=jax.ShapeDtypeStruct((r, c), BF16),
        compiler_params=_params("parallel", "parallel"),
        name="cast_bf16",
    )(w)


def _inproj_kernel(x_ref, w_ref, s_ref, o_ref, *xb_scratch):
    if xb_scratch:
        xb_ref, = xb_scratch

        @pl.when(pl.program_id(1) == 0)
        def _():
            xb_ref[...] = x_ref[...].astype(BF16)
    else:
        xb_ref = x_ref
    acc = jnp.dot(xb_ref[...], w_ref[...], preferred_element_type=F32)
    o_ref[...] = (acc * s_ref[...]).astype(o_ref.dtype)


def inproj(x, w, colscale, out_dtype, bm=1024, bn=1536):
    t, k = x.shape
    n = w.shape[1]
    assert t % bm == 0 and n % bn == 0
    return pl.pallas_call(
        _inproj_kernel,
        grid=(t // bm, n // bn),
        in_specs=[
            pl.BlockSpec((bm, k), lambda i, j: (i, 0)),
            pl.BlockSpec((k, bn), lambda i, j: (0, j)),
            pl.BlockSpec((1, bn), lambda i, j: (0, j)),
        ],
        out_specs=pl.BlockSpec((bm, bn), lambda i, j: (i, j)),
        out_shape=jax.ShapeDtypeStruct((t, n), out_dtype),
        scratch_shapes=[] if x.dtype == BF16 else [pltpu.VMEM((bm, k), BF16)],
        compiler_params=_params("parallel", "arbitrary"),
        name="inproj",
    )(x, w, colscale)


CONV_ROWS = 64


def _conv_kernel(a_ref, g_ref, ah_ref, gh_ref, w_ref, cb_ref, lg_ref, lb_ref, o_ref, hext_ref, hs_ref, y_ref, wb_ref,
                 *, bt):
    i = pl.program_id(1)
    halo = ah_ref[...] * _sigmoid(gh_ref[...])
    hext_ref[0:CONV_HALO, :] = jnp.where(i > 0, halo, 0.0)
    hext_ref[CONV_HALO:CONV_HALO + bt, :] = a_ref[...] * _sigmoid(g_ref[...])
    span = bt + CONV_HALO - SUBLANES
    for d in range(1, SUBLANES):
        hs_ref[d - 1, 0:span, :] = hext_ref[d:d + span, :]

    first = CONV_HALO - (CONV_WIDTH - 1)
    for j in range(CONV_WIDTH):
        wb_ref[j] = jnp.broadcast_to(w_ref[j:j + 1, :], (SUBLANES, BRANCH))
    wb_ref[CONV_WIDTH] = jnp.broadcast_to(cb_ref[...], (SUBLANES, BRANCH))
    groups = CONV_ROWS // SUBLANES

    def rows(c, carry):
        base = pl.multiple_of(c * CONV_ROWS, CONV_ROWS)
        for cg in range(BRANCH // 256):
            cs = slice(cg * 256, (cg + 1) * 256)
            acc = jnp.broadcast_to(wb_ref[CONV_WIDTH, :, cs][None], (groups, SUBLANES, 256))
            for j in range(CONV_WIDTH):
                off = first + j
                d, al = off % SUBLANES, off - off % SUBLANES
                if d == 0:
                    tap = hext_ref[pl.ds(base + al, CONV_ROWS), cs]
                else:
                    tap = hs_ref[d - 1, pl.ds(base + al, CONV_ROWS), cs]
                acc = acc + wb_ref[j, :, cs][None] * tap.reshape(groups, SUBLANES, 256)
            y_ref[pl.ds(base, CONV_ROWS), cs] = acc.reshape(CONV_ROWS, 256)
        return carry

    lax.fori_loop(0, bt // CONV_ROWS, rows, 0)
    z = _layer_norm_rows(y_ref[...], lg_ref[...], lb_ref[...])
    o_ref[...] = (z * _sigmoid(z)).astype(o_ref.dtype)


def conv_branch(u, conv_w, conv_b, ln_g, ln_b, bt=512):
    b, s, _ = u.shape
    assert s % bt == 0 and bt % CONV_ROWS == 0 and bt % CONV_HALO == 0
    hb = bt // CONV_HALO
    w = jnp.pad(conv_w, ((0, CONV_HALO - CONV_WIDTH), (0, 0)))
    row = lambda v: v.reshape(1, BRANCH)
    vec = pl.BlockSpec((1, BRANCH), lambda bi, i: (0, 0))
    return pl.pallas_call(
        functools.partial(_conv_kernel, bt=bt),
        grid=(b, s // bt),
        in_specs=[
            pl.BlockSpec((None, bt, BRANCH), lambda bi, i: (bi, i, 0)),
            pl.BlockSpec((None, bt, BRANCH), lambda bi, i: (bi, i, 1)),
            pl.BlockSpec((None, CONV_HALO, BRANCH), lambda bi, i: (bi, jnp.maximum(i * hb - 1, 0), 0)),
            pl.BlockSpec((None, CONV_HALO, BRANCH), lambda bi, i: (bi, jnp.maximum(i * hb - 1, 0), 1)),
            pl.BlockSpec((CONV_HALO, BRANCH), lambda bi, i: (0, 0)),
            vec, vec, vec,
        ],
        out_specs=pl.BlockSpec((None, bt, BRANCH), lambda bi, i: (bi, i, 0)),
        out_shape=jax.ShapeDtypeStruct((b, s, BRANCH), BF16),
        scratch_shapes=[
            pltpu.VMEM((bt + CONV_HALO, BRANCH), F32),
            pltpu.VMEM((SUBLANES - 1, bt + CONV_HALO - SUBLANES, BRANCH), F32),
            pltpu.VMEM((bt, BRANCH), F32),
            pltpu.VMEM((CONV_WIDTH + 1, SUBLANES, BRANCH), F32),
        ],
        compiler_params=_params("parallel", "arbitrary"),
        name="conv_branch",
    )(u, u, u, u, w, row(conv_b), row(ln_g), row(ln_b))


HG_HALF = CHUNK // 2
HG_PAIR = 2 * HEAD_DIM
HG_STEPS = 4


def _hgrn_kernel(q_ref, f_ref, v_ref, g_ref, lbl_ref, gn_ref, o_ref, state_ref, *, slot):
    @pl.when(pl.program_id(1) == 0)
    def _():
        state_ref[...] = jnp.zeros_like(state_ref)

    c, hh = CHUNK, HG_HALF
    lbl = lbl_ref[...]
    e = jnp.exp(lbl - jnp.max(lbl, axis=0, keepdims=True))
    upto = lax.broadcasted_iota(jnp.int32, e.shape, 0) <= slot
    lb = jnp.sum(jnp.where(upto, e, 0.0), axis=0, keepdims=True) / jnp.sum(e, axis=0, keepdims=True)
    r_i = lax.broadcasted_iota(jnp.int32, (c, c), 0)
    c_i = lax.broadcasted_iota(jnp.int32, (c, c), 1)
    tri = jnp.where(c_i <= r_i, 1.0, 0.0).astype(BF16)

    def gates(rows):
        f = lb + (1.0 - lb) * _sigmoid(f_ref[rows, :])
        lf = jnp.log(f)
        kk = 1.0 - f
        q = q_ref[rows, :]
        qs = q * _sigmoid(q)
        l3 = jnp.dot(tri, jnp.concatenate(_split3(lf), axis=1), preferred_element_type=F32)
        ll = l3[:, :BRANCH] + l3[:, BRANCH:2 * BRANCH] + l3[:, 2 * BRANCH:]
        l_mid0 = ll[hh // 2 - 1:hh // 2]
        l_edge = ll[hh - 1:hh]
        l_mid1 = ll[hh + hh // 2 - 1:hh + hh // 2]
        l_end = ll[c - 1:c]
        lt, lbt = ll[:hh], ll[hh:]
        return dict(
            a_d0=qs[:hh] * jnp.exp(lt - l_mid0),
            b_d0=kk[:hh] * jnp.exp(l_mid0 - lt),
            a_off=qs[hh:] * jnp.exp(lbt - l_edge),
            b_off=kk[:hh] * jnp.exp(l_edge - lt),
            a_d1=qs[hh:] * jnp.exp(lbt - l_mid1),
            b_d1=kk[hh:] * jnp.exp(l_mid1 - lbt),
            a_int=(qs * jnp.exp(ll)).astype(BF16),
            kd=(kk * jnp.exp(l_end - ll)).astype(BF16),
            dec=jnp.exp(l_end),
        )

    zero = jnp.zeros((hh, HG_PAIR), F32)
    first = lax.broadcasted_iota(jnp.int32, (c, HG_PAIR), 1) < HEAD_DIM
    first3 = jnp.concatenate([first, first, first], axis=1)
    row2 = lax.broadcasted_iota(jnp.int32, (HG_PAIR, HG_PAIR), 0) < HEAD_DIM
    col2 = lax.broadcasted_iota(jnp.int32, (HG_PAIR, HG_PAIR), 1) < HEAD_DIM
    same_head = row2 == col2
    key_i = lax.broadcasted_iota(jnp.int32, (c, 2 * c), 1) % c
    qry_i = lax.broadcasted_iota(jnp.int32, (c, 2 * c), 0)
    stack = lambda top, bottom: jnp.concatenate([top, bottom], axis=0)

    pairs = range(HEADS // 2)
    pss = [slice(p * HG_PAIR, (p + 1) * HG_PAIR) for p in pairs]

    def scores(gt):
        scs = []
        for ps in pss:
            lhs = jnp.concatenate([stack(gt["a_d0"][:, ps], zero), stack(zero, gt["a_off"][:, ps]),
                                   stack(zero, gt["a_d1"][:, ps])], axis=1)
            keys = jnp.concatenate([stack(gt["b_d0"][:, ps], zero), stack(gt["b_off"][:, ps], zero),
                                    stack(zero, gt["b_d1"][:, ps])], axis=1)
            rhs_t = stack(jnp.where(first3, keys, 0.0), jnp.where(first3, 0.0, keys))
            sc = lax.dot_general(lhs.astype(BF16), rhs_t.astype(BF16), NT_DIMS, preferred_element_type=F32)
            scs.append(jnp.where(key_i <= qry_i, sc, 0.0))
        return scs

    chunks = [slice(k * c, (k + 1) * c) for k in range(o_ref.shape[0] // c)]
    gts = [gates(rows) for rows in chunks]
    scss = [scores(gt) for gt in gts]
    states = [state_ref[p] for p in pairs]
    for rows, gt, scs in zip(chunks, gts, scss):
        inter = []
        for p in pairs:
            st = states[p]
            inter.append(lax.dot_general(gt["a_int"][:, pss[p]], st.astype(BF16), NT_DIMS, preferred_element_type=F32))
            upd = lax.dot_general(v_ref[rows, pss[p]].astype(BF16), gt["kd"][:, pss[p]], TN_DIMS,
                                  preferred_element_type=F32)
            states[p] = gt["dec"][:, pss[p]] * st + jnp.where(same_head, upd, 0.0)
        for p in pairs:
            v = v_ref[rows, pss[p]]
            v_bd = stack(jnp.where(first, v, 0.0), jnp.where(first, 0.0, v)).astype(BF16)
            o = jnp.dot(scs[p].astype(BF16), v_bd, preferred_element_type=F32) + inter[p]
            for j in range(2):
                hsl = slice(p * HG_PAIR + j * HEAD_DIM, p * HG_PAIR + (j + 1) * HEAD_DIM)
                oj = o[:, j * HEAD_DIM:(j + 1) * HEAD_DIM]
                oj = oj * lax.rsqrt(jnp.mean(oj * oj, axis=-1, keepdims=True) + LN_EPS)
                g = g_ref[rows, hsl]
                o_ref[rows, hsl] = (oj * gn_ref[:, hsl] * (g * _sigmoid(g))).astype(o_ref.dtype)
    for p in pairs:
        state_ref[p] = states[p]


def hgrn2(u, lb_logits, slot, gnorm_g):
    b, s, _ = u.shape
    slots = lb_logits.shape[0]
    rows = HG_STEPS * CHUNK
    assert s % rows == 0
    col = lambda j: pl.BlockSpec((None, rows, BRANCH), lambda bi, i: (bi, i, j))
    vec = pl.BlockSpec((1, BRANCH), lambda bi, i: (0, 0))
    return pl.pallas_call(
        functools.partial(_hgrn_kernel, slot=slot),
        grid=(b, s // rows),
        in_specs=[col(2), col(3), col(4), col(5), pl.BlockSpec((slots, BRANCH), lambda bi, i: (0, 0)), vec],
        out_specs=pl.BlockSpec((None, rows, BRANCH), lambda bi, i: (bi, i, 0)),
        out_shape=jax.ShapeDtypeStruct((b, s, BRANCH), BF16),
        scratch_shapes=[pltpu.VMEM((HEADS // 2, HG_PAIR, HG_PAIR), F32)],
        compiler_params=_params("parallel", "arbitrary"),
        name="hgrn2",
    )(u, u, u, u, lb_logits.astype(F32), gnorm_g.reshape(1, BRANCH))


def _fgate_kernel(x_ref, w_ref, b_ref, o_ref, carry_ref, *, bm):
    @pl.when(pl.program_id(1) == 0)
    def _():
        carry_ref[...] = jnp.zeros_like(carry_ref)

    z = jnp.dot(x_ref[...].astype(BF16), w_ref[...], preferred_element_type=F32) + b_ref[...]
    ls = jnp.minimum(z, 0.0) - jnp.log(1.0 + jnp.exp(-jnp.abs(z)))
    r_i = lax.broadcasted_iota(jnp.int32, (bm, bm), 0)
    c_i = lax.broadcasted_iota(jnp.int32, (bm, bm), 1)
    tri = jnp.where(c_i <= r_i, 1.0, 0.0).astype(BF16)
    hi, mid, lo = _split3(ls)
    cum = (jnp.dot(tri, hi, preferred_element_type=F32)
           + jnp.dot(tri, mid, preferred_element_type=F32)
           + jnp.dot(tri, lo, preferred_element_type=F32)) + carry_ref[...]
    carry_ref[...] = cum[bm - 1:bm, :]
    neg = cum * (-LOG2E)
    lane = lax.broadcasted_iota(jnp.int32, (bm, LANES), 1)
    for h in range(HEADS):
        hi, mid, lo = (t.astype(F32) for t in _split3(jnp.broadcast_to(neg[:, h:h + 1], (bm, LANES))))
        o_ref[h] = jnp.where(lane == 0, hi, jnp.where(lane == 1, mid, jnp.where(lane == 2, lo, 0.0))).astype(BF16)


def forget_cumsum(x, w_f, b_f, bm=512):
    b, s, d = x.shape
    assert s % bm == 0
    w = jnp.pad(w_f, ((0, 0), (0, LANES - HEADS))).astype(BF16)
    bias = jnp.pad(b_f, (0, LANES - HEADS)).reshape(1, LANES).astype(F32)
    return pl.pallas_call(
        functools.partial(_fgate_kernel, bm=bm),
        grid=(b, s // bm),
        in_specs=[
            pl.BlockSpec((None, bm, d), lambda bi, i: (bi, i, 0)),
            pl.BlockSpec((d, LANES), lambda bi, i: (0, 0)),
            pl.BlockSpec((1, LANES), lambda bi, i: (0, 0)),
        ],
        out_specs=pl.BlockSpec((None, HEADS, bm, LANES), lambda bi, i: (bi, 0, i, 0)),
        out_shape=jax.ShapeDtypeStruct((b, HEADS, s, LANES), BF16),
        scratch_shapes=[pltpu.VMEM((1, LANES), F32)],
        compiler_params=_params("parallel", "arbitrary"),
        name="forget_cumsum",
    )(x, w, bias)


FOX_GROUP = 4


def _fox_kernel(q_ref, k_ref, v_ref, f_ref, o_ref, vt_ref, acc_ref, *, bq):
    qi = pl.program_id(2)
    nblk = pl.num_programs(2)
    d = HEAD_DIM
    heads = range(FOX_GROUP)
    hs = [slice(h * d, (h + 1) * d) for h in heads]

    @pl.when(qi == 0)
    def _():
        ones_row = jnp.where(lax.broadcasted_iota(jnp.int32, (ONES_PAD, bq), 0) == 0, 1.0, 0.0).astype(BF16)

        def tr(c, carry):
            cs = pl.multiple_of(c * bq, bq)
            for h in heads:
                vt_ref[h, :d, pl.ds(cs, bq)] = v_ref[pl.ds(cs, bq), hs[h]].T
                vt_ref[h, d:, pl.ds(cs, bq)] = ones_row
            return carry
        lax.fori_loop(0, nblk, tr, 0)

    acc_ref[...] = jnp.zeros_like(acc_ref)

    ones3 = jnp.where(lax.broadcasted_iota(jnp.int32, (bq, LANES), 1) < F_TERMS, 1.0, 0.0).astype(BF16)
    q_aug = [jnp.concatenate([q_ref[:, hs[h]], ones3], axis=1) for h in heads]

    def step(ks, nk, carry, causal_from):
        sts = []
        for h in heads:
            k_aug = jnp.concatenate([k_ref[pl.ds(ks, nk), hs[h]], f_ref[h, pl.ds(ks, nk), :]], axis=1)
            sts.append(lax.dot_general(k_aug, q_aug[h], NT_DIMS, preferred_element_type=F32))
        if causal_from is not None:
            keep = (lax.broadcasted_iota(jnp.int32, (nk, bq), 0) - causal_from
                    <= lax.broadcasted_iota(jnp.int32, (nk, bq), 1))
            sts = [jnp.where(keep, st, NEG) for st in sts]
        out = []
        for h in heads:
            m = carry[h]
            m_new = jnp.maximum(m, jnp.max(sts[h], axis=0, keepdims=True))
            p = jnp.exp2(sts[h] - m_new).astype(BF16)
            pv = jnp.dot(vt_ref[h, :, pl.ds(ks, nk)], p, preferred_element_type=F32)
            acc_ref[h] = jnp.exp2(m - m_new) * acc_ref[h] + pv
            out.append(m_new)
        return tuple(out)

    def finish(ks, nk, carry, causal_from):
        step(ks, nk, carry, causal_from)
        for h in heads:
            o_ref[:, hs[h]] = (acc_ref[h, :d, :] / acc_ref[h, d:d + 1, :]).astype(o_ref.dtype).T

    init = tuple(jnp.full((1, bq), NEG, F32) for _ in heads)
    carry = lax.fori_loop(0, qi // 2, lambda t, cr: step(pl.multiple_of(t * 2 * bq, 2 * bq), 2 * bq, cr, None), init)

    @pl.when(qi % 2 == 0)
    def _():
        finish(pl.multiple_of(qi * bq, bq), bq, carry, 0)

    @pl.when(qi % 2 == 1)
    def _():
        finish(pl.multiple_of((qi - 1) * bq, 2 * bq), 2 * bq, carry, bq)


def fox_attention(qkv, fcum, q_col, k_col, v_col, bq=512):
    b, s, _ = qkv.shape
    g = FOX_GROUP
    w = g * HEAD_DIM
    assert q_col % g == 0 and k_col % g == 0 and v_col % g == 0 and HEADS % g == 0 and s % bq == 0
    return pl.pallas_call(
        functools.partial(_fox_kernel, bq=bq),
        grid=(b, HEADS // g, s // bq),
        in_specs=[
            pl.BlockSpec((None, bq, w), lambda bi, h, i: (bi, i, q_col // g + h)),
            pl.BlockSpec((None, s, w), lambda bi, h, i: (bi, 0, k_col // g + h), pipeline_mode=pl.Buffered(1)),
            pl.BlockSpec((None, s, w), lambda bi, h, i: (bi, 0, v_col // g + h), pipeline_mode=pl.Buffered(1)),
            pl.BlockSpec((None, g, s, LANES), lambda bi, h, i: (bi, h, 0, 0), pipeline_mode=pl.Buffered(1)),
        ],
        out_specs=pl.BlockSpec((None, bq, w), lambda bi, h, i: (bi, i, h)),
        out_shape=jax.ShapeDtypeStruct((b, s, BRANCH), BF16),
        scratch_shapes=[pltpu.VMEM((g, HEAD_DIM + ONES_PAD, s), BF16), pltpu.VMEM((g, HEAD_DIM + ONES_PAD, bq), F32)],
        compiler_params=_params("parallel", "parallel", "arbitrary"),
        name="fox_attention",
    )(qkv, qkv, qkv, fcum)


CA_GROUP = 4


def _chunk_attn_kernel(q_ref, kp_ref, kc_ref, vp_ref, vc_ref, row_ref, o_ref, bias_ref, *, bq):
    qi = pl.program_id(2)
    d = HEAD_DIM
    hq = bq // 2
    heads = range(CA_GROUP)
    hs = [slice(h * d, (h + 1) * d) for h in heads]

    @pl.when((pl.program_id(1) == 0) & (qi == 0))
    def _():
        w = 2 * bq
        qc = lax.broadcasted_iota(jnp.int32, (hq, w), 0) // CHUNK
        kc = lax.broadcasted_iota(jnp.int32, (hq, w), 1) // CHUNK - bq // CHUNK
        key_row = lax.broadcasted_iota(jnp.int32, (3 * hq, hq), 0)
        for h in heads:
            toeplitz = pltpu.roll(jnp.broadcast_to(row_ref[h], (hq, w)), 0, 1, stride=1, stride_axis=0)
            band = jnp.where(kc <= qc, jnp.where(kc >= qc - CA_LEFT_CHUNKS, toeplitz, NEG), NEG)
            tile = band[:, :3 * hq].T
            bias_ref[0, h] = tile
            bias_ref[1, h] = jnp.where(key_row >= bq, tile, NEG)
            bias_ref[2, h] = jnp.where(key_row >= hq, tile, NEG)

    variant = [jnp.where(qi == 0, 1 + j, 0) for j in range(2)]
    ones_rows = jnp.where(lax.broadcasted_iota(jnp.int32, (ONES_PAD, bq), 0) == 0, 1.0, 0.0).astype(BF16)

    sts = []
    for h in heads:
        kp, kc_ = kp_ref[:, hs[h]], kc_ref[:, hs[h]]
        windows = [jnp.concatenate([kp, kc_[:hq]], axis=0), jnp.concatenate([kp[hq:], kc_], axis=0)]
        for j in range(2):
            sts.append(lax.dot_general(windows[j], q_ref[j * hq:(j + 1) * hq, hs[h]], NT_DIMS,
                                       preferred_element_type=F32) + bias_ref[variant[j], h])
    for h in heads:
        vtp = jnp.concatenate([vp_ref[:, hs[h]].T, ones_rows], axis=0)
        vtc = jnp.concatenate([vc_ref[:, hs[h]].T, ones_rows], axis=0)
        windows = [jnp.concatenate([vtp, vtc[:, :hq]], axis=1), jnp.concatenate([vtp[:, hq:], vtc], axis=1)]
        for j in range(2):
            s = sts[2 * h + j]
            p = jnp.exp(s - jnp.max(s, axis=0, keepdims=True)).astype(BF16)
            o = jnp.dot(windows[j], p, preferred_element_type=F32)
            o_ref[j * hq:(j + 1) * hq, hs[h]] = (o[:d] / o[d:d + 1]).astype(o_ref.dtype).T


def _distance_rows(rel_bias, bq):
    heads, table = rel_bias.shape
    assert table == (CHUNK - 1) + REL_CLIP + 1 and bq >= REL_CLIP
    far = rel_bias[:, table - 1:]
    return jnp.concatenate([
        jnp.broadcast_to(far, (heads, bq - REL_CLIP)),
        rel_bias[:, ::-1],
        jnp.broadcast_to(far, (heads, bq - CHUNK)),
    ], axis=1).astype(F32).reshape(heads, 1, 2 * bq)


def chunk_attention(qkv, rel_bias, q_col, k_col, v_col, bq=512):
    b, s, _ = qkv.shape
    assert bq == CA_LEFT_CHUNKS * CHUNK
    rows = _distance_rows(rel_bias, bq)
    g = CA_GROUP
    w = g * HEAD_DIM
    assert q_col % g == 0 and k_col % g == 0 and v_col % g == 0 and HEADS % g == 0 and s % bq == 0
    cur = lambda col: pl.BlockSpec((None, bq, w), lambda h, bi, i: (bi, i, col // g + h))
    prev = lambda col: pl.BlockSpec((None, bq, w), lambda h, bi, i: (bi, jnp.maximum(i - 1, 0), col // g + h))
    return pl.pallas_call(
        functools.partial(_chunk_attn_kernel, bq=bq),
        grid=(HEADS // g, b, s // bq),
        in_specs=[cur(q_col), prev(k_col), cur(k_col), prev(v_col), cur(v_col),
                  pl.BlockSpec((g, 1, 2 * bq), lambda h, bi, i: (h, 0, 0))],
        out_specs=pl.BlockSpec((None, bq, w), lambda h, bi, i: (bi, i, h)),
        out_shape=jax.ShapeDtypeStruct((b, s, BRANCH), BF16),
        scratch_shapes=[pltpu.VMEM((3, g, 3 * bq // 2, bq // 2), F32)],
        compiler_params=_params("arbitrary", "arbitrary", "arbitrary"),
        name="chunk_attention",
    )(qkv, qkv, qkv, qkv, qkv, rows)


def _outproj_ln_kernel(a_ref, b_ref, wa_ref, wb_ref, x_ref, g_ref, beta_ref, o_ref):
    mix = (jnp.dot(a_ref[...], wa_ref[...], preferred_element_type=F32)
           + jnp.dot(b_ref[...], wb_ref[...], preferred_element_type=F32))
    o_ref[...] = _layer_norm_rows(ALPHA * x_ref[...] + mix, g_ref[...], beta_ref[...])


def outproj_ln(a, b, w_out, x, g, beta, bm=512):
    t, d = x.shape
    assert t % bm == 0 and w_out.shape == (2 * BRANCH, d)
    vec = pl.BlockSpec((1, d), lambda i: (0, 0))
    act = pl.BlockSpec((bm, BRANCH), lambda i: (i, 0))
    return pl.pallas_call(
        _outproj_ln_kernel,
        grid=(t // bm,),
        in_specs=[act, act,
                  pl.BlockSpec((BRANCH, d), lambda i: (0, 0)), pl.BlockSpec((BRANCH, d), lambda i: (1, 0)),
                  pl.BlockSpec((bm, d), lambda i: (i, 0)), vec, vec],
        out_specs=pl.BlockSpec((bm, d), lambda i: (i, 0)),
        out_shape=jax.ShapeDtypeStruct((t, d), F32),
        compiler_params=_params("parallel"),
        name="outproj_ln",
    )(a, b, w_out, w_out, x, g.reshape(1, d), beta.reshape(1, d))


def _mlp_ln_kernel(x_ref, w1_ref, w2_ref, g_ref, beta_ref, o_ref, *rest):
    *ob_refs, xb_ref, acc_ref = rest
    j = pl.program_id(1)

    @pl.when(j == 0)
    def _():
        xb_ref[...] = x_ref[...].astype(BF16)
        acc_ref[...] = jnp.zeros_like(acc_ref)

    h = jnp.maximum(jnp.dot(xb_ref[...], w1_ref[...], preferred_element_type=F32), 0.0)
    acc_ref[...] += jnp.dot((h * h).astype(BF16), w2_ref[...], preferred_element_type=F32)

    @pl.when(j == pl.num_programs(1) - 1)
    def _():
        y = _layer_norm_rows(ALPHA * x_ref[...] + acc_ref[...], g_ref[...], beta_ref[...])
        o_ref[...] = y
        for ob_ref in ob_refs:
            ob_ref[...] = y.astype(BF16)


def mlp_ln(x, w1, w2, g, beta, also_bf16=False, bm=512, bf=1024):
    t, d = x.shape
    ff = w1.shape[1]
    assert t % bm == 0 and ff % bf == 0
    vec = pl.BlockSpec((1, d), lambda i, j: (0, 0))
    row_block = pl.BlockSpec((bm, d), lambda i, j: (i, 0))
    out_specs, out_shape = row_block, jax.ShapeDtypeStruct((t, d), F32)
    if also_bf16:
        out_specs, out_shape = [row_block, row_block], [out_shape, jax.ShapeDtypeStruct((t, d), BF16)]
    return pl.pallas_call(
        _mlp_ln_kernel,
        grid=(t // bm, ff // bf),
        in_specs=[
            pl.BlockSpec((bm, d), lambda i, j: (i, 0)),
            pl.BlockSpec((d, bf), lambda i, j: (0, j)),
            pl.BlockSpec((bf, d), lambda i, j: (j, 0)),
            vec, vec,
        ],
        out_specs=out_specs,
        out_shape=out_shape,
        scratch_shapes=[pltpu.VMEM((bm, d), BF16), pltpu.VMEM((bm, d), F32)],
        compiler_params=_params("parallel", "arbitrary"),
        name="mlp_ln",
    )(x, w1, w2, g.reshape(1, d), beta.reshape(1, d))


def even_mixer(x, w_in, conv_w, conv_b, conv_ln_g, conv_ln_b, lb_logits, slot, gnorm_g):
    b, s, d = x.shape
    n = w_in.shape[1]
    u = inproj(x.reshape(b * s, d), w_in, jnp.ones((1, n), F32), F32).reshape(b, s, n)
    a_out = conv_branch(u, conv_w, conv_b, conv_ln_g, conv_ln_b)
    b_out = hgrn2(u, lb_logits, slot, gnorm_g)
    return a_out.reshape(b * s, BRANCH), b_out.reshape(b * s, BRANCH)


def odd_mixer(x, w_in, b_f, rel_bias):
    b, s, d = x.shape
    w_qkv = jnp.concatenate([w_in[:, :3 * BRANCH], w_in[:, 3 * BRANCH + HEADS:]], axis=1).astype(BF16)
    w_f = w_in[:, 3 * BRANCH:3 * BRANCH + HEADS]
    scale = HEAD_DIM ** -0.5
    ones = jnp.ones((BRANCH,), F32)
    colscale = jnp.concatenate([ones * (scale * LOG2E), ones, ones, ones * scale, ones, ones]).reshape(1, 6 * BRANCH)
    qkv = inproj(x.reshape(b * s, d), w_qkv, colscale, BF16).reshape(b, s, 6 * BRANCH)
    fcum = forget_cumsum(x, w_f, b_f)
    c_out = fox_attention(qkv, fcum, 0, HEADS, 2 * HEADS)
    d_out = chunk_attention(qkv, rel_bias, 3 * HEADS, 4 * HEADS, 5 * HEADS)
    return c_out.reshape(b * s, BRANCH), d_out.reshape(b * s, BRANCH)


def kernel(x, ev_w_in, ev_conv_w, ev_conv_b, ev_conv_ln_g, ev_conv_ln_b, hgrn_lb_logits, ev_gnorm_g, ev_w_out, od_w_in, fox_b_f, rel_bias, od_w_out, ln_mix_g, ln_mix_b, mlp_w1, mlp_w2, ln_mlp_g, ln_mlp_b):
    b, s, d = x.shape
    x_mix = x
    for l in range(DEPTH):
        j = l // 2
        if l % 2 == 0:
            p, r = even_mixer(x_mix, cast_layer_bf16(ev_w_in, j), ev_conv_w[j], ev_conv_b[j], ev_conv_ln_g[j],
                              ev_conv_ln_b[j], hgrn_lb_logits, j, ev_gnorm_g[j])
            w_out = cast_layer_bf16(ev_w_out, j)
        else:
            p, r = odd_mixer(x_mix, od_w_in[j], fox_b_f[j], rel_bias[j])
            w_out = cast_layer_bf16(od_w_out, j)
        x2 = outproj_ln(p, r, w_out, x.reshape(b * s, d), ln_mix_g[l], ln_mix_b[l])
        w1, w2 = cast_layer_bf16(mlp_w1, l), cast_layer_bf16(mlp_w2, l)
        if l + 1 < DEPTH:
            x2, xb = mlp_ln(x2, w1, w2, ln_mlp_g[l], ln_mlp_b[l], also_bf16=True)
            x_mix = xb.reshape(b, s, d)
        else:
            x2 = mlp_ln(x2, w1, w2, ln_mlp_g[l], ln_mlp_b[l])
        x = x2.reshape(b, s, d)
    return x
```
